```python
import math
import jax, jax.numpy as jnp
from jax import lax
import numpy as np

D_MODEL = 1024
BATCH = 8
SEQ = 2048
DEPTH = 4

N_META = 16
CHUNK = 128
NORM_EPS = 1e-6

SSD_HEADS = 4
SSD_HEAD_DIM = 64
SSD_WIDTH = SSD_HEADS * SSD_HEAD_DIM
SSD_GROUPS = 2
SSD_STATE = 128
SSD_CONV = 4
SSD_CONV_CH = SSD_WIDTH + 2 * SSD_GROUPS * SSD_STATE
SSD_IN = SSD_WIDTH + SSD_CONV_CH + SSD_HEADS

RWKV_HEADS = 4
RWKV_HEAD_DIM = 64
RWKV_WIDTH = RWKV_HEADS * RWKV_HEAD_DIM
RWKV_DECAY_RANK = 64
RWKV_A_RANK = 64
RWKV_GATE_RANK = 128
RWKV_IN = 3 * RWKV_WIDTH + RWKV_DECAY_RANK + RWKV_A_RANK + RWKV_GATE_RANK
RWKV_GN_EPS = 64e-5

LRU_BLOCKS = 4
LRU_BLOCK_DIM = 64
LRU_WIDTH = LRU_BLOCKS * LRU_BLOCK_DIM
LRU_CONV = 4
LRU_C = 8.0
LRU_IN = 2 * LRU_WIDTH

RET_HEADS = 4
RET_QK_DIM = 32
RET_V_DIM = 64
RET_WIDTH = RET_HEADS * RET_V_DIM
RET_IN = 2 * RET_HEADS * RET_QK_DIM + 2 * RET_WIDTH
RET_GN_EPS = 1e-5
ROPE_BASE = 10000.0

MIX_IN = SSD_IN + RWKV_IN + LRU_IN + RET_IN
MIX_WIDTH = SSD_WIDTH + RWKV_WIDTH + LRU_WIDTH + RET_WIDTH
D_FF = -(-8 * D_MODEL // (3 * 256)) * 256

kernel_name = "hybrid_ssd_rwkv7_rglru_retention_trunk"


def split_last(x, sizes):
    idx = [int(s) for s in np.cumsum(sizes)[:-1]]
    return jnp.split(x, idx, axis=-1)


def rms_norm(x, w):
    x32 = x.astype(jnp.float32)
    y = x32 * lax.rsqrt(jnp.mean(x32 * x32, axis=-1, keepdims=True) + NORM_EPS)
    return (y * w.astype(jnp.float32)).astype(x.dtype)


def causal_depthwise_conv(x, w, b):
    K, C = w.shape
    y = lax.conv_general_dilated(x, w[:, None, :].astype(x.dtype), window_strides=(1,),
                                 padding=[(K - 1, 0)], dimension_numbers=('NWC', 'WIO', 'NWC'),
                                 feature_group_count=C)
    return y + b.astype(x.dtype)


def pad_front(x, n):
    return jnp.pad(x, [(0, 0), (n, 0)] + [(0, 0)] * (x.ndim - 2))


def segsum(a):
    L = a.shape[-1]
    cs = jnp.cumsum(a, axis=-1)
    diff = cs[..., :, None] - cs[..., None, :]
    mask = jnp.tril(jnp.ones((L, L), dtype=bool))
    return jnp.where(mask, diff, -jnp.inf)


def head_group_norm(y, eps):
    mu = jnp.mean(y, axis=-1, keepdims=True)
    var = jnp.mean(jnp.square(y - mu), axis=-1, keepdims=True)
    return (y - mu) * lax.rsqrt(var + eps)


def rope(x, pos):
    half = x.shape[-1] // 2
    freqs = ROPE_BASE ** (-jnp.arange(half, dtype=jnp.float32) / half)
    ang = pos.astype(jnp.float32)[:, None] * freqs[None, :]
    cos = jnp.cos(ang)[None, :, None, :]
    sin = jnp.sin(ang)[None, :, None, :]
    x1, x2 = x[..., :half], x[..., half:]
    return jnp.concatenate([x1 * cos - x2 * sin, x1 * sin + x2 * cos], axis=-1)


def ssd_chunked(x, a, b, c):
    Bsz, T, H, P = x.shape
    G = b.shape[2]
    nc = T // CHUNK
    b = jnp.repeat(b, H // G, axis=2).reshape(Bsz, nc, CHUNK, H, -1)
    c = jnp.repeat(c, H // G, axis=2).reshape(Bsz, nc, CHUNK, H, -1)
    x = x.reshape(Bsz, nc, CHUNK, H, P)
    a = a.reshape(Bsz, nc, CHUNK, H).transpose(0, 3, 1, 2)
    a_cs = jnp.cumsum(a, axis=-1)
    Lmat = jnp.exp(segsum(a))
    y_diag = jnp.einsum('bclhn,bcshn,bhcls,bcshp->bclhp', c, b, Lmat, x)
    decay_states = jnp.exp(a_cs[..., -1:] - a_cs)
    states = jnp.einsum('bclhn,bhcl,bclhp->bchpn', b, decay_states, x)
    states = jnp.pad(states, [(0, 0), (1, 0), (0, 0), (0, 0), (0, 0)])
    chunk_a = jnp.pad(a_cs[..., -1], [(0, 0), (0, 0), (1, 0)])
    decay_chunk = jnp.exp(segsum(chunk_a))
    entering = jnp.einsum('bhzc,bchpn->bzhpn', decay_chunk, states)[:, :-1]
    y_off = jnp.einsum('bclhn,bchpn,bhcl->bclhp', c, entering, jnp.exp(a_cs))
    return (y_diag + y_off).reshape(Bsz, T, H, P)


def ssd_mix(p, conv_w, conv_b, dt_bias, a_log, d_skip, norm_w):
    f32 = jnp.float32
    Bsz, T, _ = p.shape
    z, xbc, dt_raw = split_last(p, [SSD_WIDTH, SSD_CONV_CH, SSD_HEADS])
    xbc = jax.nn.silu(causal_depthwise_conv(xbc, conv_w, conv_b)).astype(f32)
    xs, bs, cs = split_last(xbc, [SSD_WIDTH, SSD_GROUPS * SSD_STATE, SSD_GROUPS * SSD_STATE])
    xs = xs.reshape(Bsz, T, SSD_HEADS, SSD_HEAD_DIM)
    bs = bs.reshape(Bsz, T, SSD_GROUPS, SSD_STATE)
    cs = cs.reshape(Bsz, T, SSD_GROUPS, SSD_STATE)
    dt = jax.nn.softplus(dt_raw.astype(f32) + dt_bias.astype(f32))
    A = -jnp.exp(a_log.astype(f32))
    pad = CHUNK - N_META
    y = ssd_chunked(pad_front(xs * dt[..., None], pad), pad_front(dt * A, pad),
                    pad_front(bs, pad), pad_front(cs, pad))[:, pad:]
    y = y + d_skip.astype(f32)[:, None] * xs
    y = y.reshape(Bsz, T, SSD_WIDTH) * jax.nn.silu(z.astype(f32))
    yg = y.reshape(Bsz, T, SSD_GROUPS, -1)
    yg = yg * lax.rsqrt(jnp.mean(yg * yg, axis=-1, keepdims=True) + NORM_EPS)
    return yg.reshape(Bsz, T, SSD_WIDTH) * norm_w.astype(f32)


def rwkv7_scan(r, w, k, v, kk, a):
    Bsz, T, H, N = r.shape

    def step(S, inp):
        r_t, w_t, k_t, v_t, kk_t, a_t = inp
        sa = jnp.einsum('bhvk,bhk->bhv', S, -kk_t)
        S = (S * w_t[:, :, None, :] + sa[..., None] * (kk_t * a_t)[:, :, None, :]
             + v_t[..., None] * k_t[:, :, None, :])
        return S, jnp.einsum('bhvk,bhk->bhv', S, r_t)

    xs = tuple(jnp.moveaxis(t, 1, 0) for t in (r, w, k, v, kk, a))
    _, y = lax.scan(step, jnp.zeros((Bsz, H, N, N), jnp.float32), xs)
    return jnp.moveaxis(y, 0, 1)


def rwkv7_mix(p, mu, w0, w2, a0, a2, g2, k_k, k_a, r_k, ln_w, ln_b):
    f32 = jnp.float32
    p = p.astype(f32)
    Bsz, T, _ = p.shape
    p_prev = jnp.pad(p, [(0, 0), (1, 0), (0, 0)])[:, :-1]
    p = p + (p_prev - p) * mu.astype(f32)
    r, k, v, w_lat, a_lat, g_lat = split_last(
        p, [RWKV_WIDTH, RWKV_WIDTH, RWKV_WIDTH, RWKV_DECAY_RANK, RWKV_A_RANK, RWKV_GATE_RANK])
    w = -jax.nn.softplus(-(w0.astype(f32) + jnp.tanh(w_lat) @ w2.astype(f32))) - 0.5
    decay = jnp.exp(-jnp.exp(w))
    a = jax.nn.sigmoid(a0.astype(f32) + a_lat @ a2.astype(f32))
    g = jax.nn.sigmoid(g_lat) @ g2.astype(f32)
    heads = lambda t: t.reshape(Bsz, T, RWKV_HEADS, RWKV_HEAD_DIM)
    kk = heads(k * k_k.astype(f32))
    kk = kk / jnp.maximum(jnp.sqrt(jnp.sum(kk * kk, axis=-1, keepdims=True)), 1e-12)
    k = k * (1.0 + (a - 1.0) * k_a.astype(f32))
    r_h, k_h, v_h, a_h = heads(r), heads(k), heads(v), heads(a)
    y = rwkv7_scan(r_h, heads(decay), k_h, v_h, kk, a_h)
    y = head_group_norm(y, RWKV_GN_EPS).reshape(Bsz, T, RWKV_WIDTH) * ln_w.astype(f32) + ln_b.astype(f32)
    bonus = jnp.sum(r_h * k_h * r_k.astype(f32), axis=-1, keepdims=True) * v_h
    y = y + bonus.reshape(Bsz, T, RWKV_WIDTH)
    return y * g


def rglru_mix(p, conv_w, conv_b, wa, ba, wx, bx, lam):
    f32 = jnp.float32
    Bsz, T, _ = p.shape
    xb, gb = split_last(p, [LRU_WIDTH, LRU_WIDTH])
    xc = causal_depthwise_conv(xb, conv_w, conv_b).astype(f32)
    xh = xc.reshape(Bsz, T, LRU_BLOCKS, LRU_BLOCK_DIM)
    r = jax.nn.sigmoid(jnp.einsum('btgi,gij->btgj', xh, wa.astype(f32)).reshape(Bsz, T, LRU_WIDTH) + ba.astype(f32))
    i = jax.nn.sigmoid(jnp.einsum('btgi,gij->btgj', xh, wx.astype(f32)).reshape(Bsz, T, LRU_WIDTH) + bx.astype(f32))
    log_a = -LRU_C * r * jax.nn.softplus(-lam.astype(f32))
    a = jnp.exp(log_a)
    u = jnp.sqrt(-jnp.expm1(2.0 * log_a)) * (i * xc)

    def combine(e1, e2):
        a1, b1 = e1
        a2, b2 = e2
        return a1 * a2, a2 * b1 + b2

    _, h = lax.associative_scan(combine, (a, u), axis=1)
    return h * jax.nn.gelu(gb.astype(f32), approximate=True)


def retention_mix(p, gn_w):
    f32 = jnp.float32
    p = p.astype(f32)
    Bsz, T, _ = p.shape
    qk = RET_HEADS * RET_QK_DIM
    q, k, v, g = split_last(p, [qk, qk, RET_WIDTH, RET_WIDTH])
    pos = jnp.arange(T)
    q = rope(q.reshape(Bsz, T, RET_HEADS, RET_QK_DIM), pos)
    k = rope(k.reshape(Bsz, T, RET_HEADS, RET_QK_DIM), pos) * (RET_QK_DIM ** -0.5)
    v = v.reshape(Bsz, T, RET_HEADS, RET_V_DIM)
    pad = CHUNK - N_META
    Tp = T + pad
    nc = Tp // CHUNK
    qc = pad_front(q, pad).reshape(Bsz, nc, CHUNK, RET_HEADS, RET_QK_DIM)
    kc = pad_front(k, pad).reshape(Bsz, nc, CHUNK, RET_HEADS, RET_QK_DIM)
    vc = pad_front(v, pad).reshape(Bsz, nc, CHUNK, RET_HEADS, RET_V_DIM)
    log_g = jnp.log1p(-jnp.exp2(-5.0 - jnp.arange(RET_HEADS, dtype=f32)))
    idx = jnp.arange(CHUNK)
    rel = idx[:, None] - idx[None, :]
    inner_decay = jnp.where(rel >= 0, jnp.exp(jnp.maximum(rel, 0)[None] * log_g[:, None, None]), 0.0)
    scores = jnp.einsum('bclhd,bcshd->bchls', qc, kc) * inner_decay
    y_inner = jnp.einsum('bchls,bcshe->bclhe', scores, vc)
    k_decay = jnp.exp((CHUNK - 1 - idx)[None, :] * log_g[:, None])
    kv = jnp.einsum('bcshd,hs,bcshe->bchde', kc, k_decay, vc)
    cidx = jnp.arange(nc)
    crel = cidx[:, None] - cidx[None, :] - 1
    cross_decay = jnp.where(crel >= 0, jnp.exp(jnp.maximum(crel, 0)[None] * (CHUNK * log_g)[:, None, None]), 0.0)
    R = jnp.einsum('hzc,bchde->bzhde', cross_decay, kv)
    q_decay = jnp.exp((idx + 1)[None, :] * log_g[:, None])
    y_cross = jnp.einsum('bclhd,bchde,hl->bclhe', qc, R, q_decay)
    y = (y_inner + y_cross).reshape(Bsz, Tp, RET_HEADS, RET_V_DIM)[:, pad:]
    y = head_group_norm(y, RET_GN_EPS).reshape(Bsz, T, RET_WIDTH) * gn_w.astype(f32)
    return y * jax.nn.silu(g)


def setup_inputs(seed: int = 0) -> dict:
    key = jax.random.key(seed)
    ks = iter(jax.random.split(key, 48))
    f32 = jnp.float32
    nrm = lambda shape, scale: jax.random.normal(next(ks), shape, f32) * scale
    unif = lambda shape, lo, hi: jax.random.uniform(next(ks), shape, f32, lo, hi)
    L = DEPTH
    x = nrm((BATCH, SEQ, D_MODEL), 1.0)
    meta_tokens = nrm((N_META, D_MODEL), 1.0)
    pre_mix_norm = 1.0 + nrm((L, D_MODEL), 0.02)
    post_mix_norm = 1.0 + nrm((L, D_MODEL), 0.02)
    pre_ffn_norm = 1.0 + nrm((L, D_MODEL), 0.02)
    post_ffn_norm = 1.0 + nrm((L, D_MODEL), 0.02)
    w_in = nrm((L, D_MODEL, MIX_IN), D_MODEL ** -0.5)
    w_out = nrm((L, MIX_WIDTH, D_MODEL), MIX_WIDTH ** -0.5)
    ssd_conv_w = nrm((L, SSD_CONV, SSD_CONV_CH), SSD_CONV ** -0.5)
    ssd_conv_b = nrm((L, SSD_CONV_CH), 0.02)
    dt = jnp.exp(unif((L, SSD_HEADS), math.log(1e-3), math.log(1e-1)))
    ssd_dt_bias = dt + jnp.log(-jnp.expm1(-dt))
    ssd_a_log = jnp.log(unif((L, SSD_HEADS), 1.0, 16.0))
    ssd_d = 1.0 + nrm((L, SSD_HEADS), 0.1)
    ssd_norm_w = 1.0 + nrm((L, SSD_WIDTH), 0.02)
    rwkv_mu = unif((L, RWKV_IN), 0.0, 1.0)
    ratio = jnp.linspace(0.0, 1.0, RWKV_WIDTH, dtype=f32)
    rwkv_w0 = (-6.5 + 5.0 * ratio ** 0.85)[None, :] + nrm((L, RWKV_WIDTH), 0.1)
    rwkv_w2 = nrm((L, RWKV_DECAY_RANK, RWKV_WIDTH), 0.1 * RWKV_DECAY_RANK ** -0.5)
    rwkv_a0 = nrm((L, RWKV_WIDTH), 0.1)
    rwkv_a2 = nrm((L, RWKV_A_RANK, RWKV_WIDTH), 0.1 * RWKV_A_RANK ** -0.5)
    rwkv_g2 = nrm((L, RWKV_GATE_RANK, RWKV_WIDTH), RWKV_GATE_RANK ** -0.5)
    rwkv_k_k = 0.85 + nrm((L, RWKV_WIDTH), 0.02)
    rwkv_k_a = 1.0 + nrm((L, RWKV_WIDTH), 0.02)
    rwkv_r_k = nrm((L, RWKV_HEADS, RWKV_HEAD_DIM), 0.1)
    rwkv_ln_w = 1.0 + nrm((L, RWKV_WIDTH), 0.02)
    rwkv_ln_b = nrm((L, RWKV_WIDTH), 0.02)
    lru_conv_w = nrm((L, LRU_CONV, LRU_WIDTH), LRU_CONV ** -0.5)
    lru_conv_b = nrm((L, LRU_WIDTH), 0.02)
    lru_wa = nrm((L, LRU_BLOCKS, LRU_BLOCK_DIM, LRU_BLOCK_DIM), LRU_BLOCK_DIM ** -0.5)
    lru_ba = nrm((L, LRU_WIDTH), 0.02)
    lru_wx = nrm((L, LRU_BLOCKS, LRU_BLOCK_DIM, LRU_BLOCK_DIM), LRU_BLOCK_DIM ** -0.5)
    lru_bx = nrm((L, LRU_WIDTH), 0.02)
    s = unif((L, LRU_WIDTH), 0.9, 0.999) ** (1.0 / LRU_C)
    lru_lambda = jnp.log(s) - jnp.log1p(-s)
    ret_gn_w = 1.0 + nrm((L, RET_WIDTH), 0.02)
    ffn_w_gate = nrm((L, D_MODEL, D_FF), D_MODEL ** -0.5)
    ffn_w_up = nrm((L, D_MODEL, D_FF), D_MODEL ** -0.5)
    ffn_w_down = nrm((L, D_FF, D_MODEL), D_FF ** -0.5)
    return {"x": x, "meta_tokens": meta_tokens, "pre_mix_norm": pre_mix_norm,
            "post_mix_norm": post_mix_norm, "pre_ffn_norm": pre_ffn_norm,
            "post_ffn_norm": post_ffn_norm, "w_in": w_in, "w_out": w_out,
            "ssd_conv_w": ssd_conv_w, "ssd_conv_b": ssd_conv_b, "ssd_dt_bias": ssd_dt_bias,
            "ssd_a_log": ssd_a_log, "ssd_d": ssd_d, "ssd_norm_w": ssd_norm_w,
            "rwkv_mu": rwkv_mu, "rwkv_w0": rwkv_w0, "rwkv_w2": rwkv_w2, "rwkv_a0": rwkv_a0,
            "rwkv_a2": rwkv_a2, "rwkv_g2": rwkv_g2, "rwkv_k_k": rwkv_k_k, "rwkv_k_a": rwkv_k_a,
            "rwkv_r_k": rwkv_r_k, "rwkv_ln_w": rwkv_ln_w, "rwkv_ln_b": rwkv_ln_b,
            "lru_conv_w": lru_conv_w, "lru_conv_b": lru_conv_b, "lru_wa": lru_wa,
            "lru_ba": lru_ba, "lru_wx": lru_wx, "lru_bx": lru_bx, "lru_lambda": lru_lambda,
            "ret_gn_w": ret_gn_w, "ffn_w_gate": ffn_w_gate, "ffn_w_up": ffn_w_up,
            "ffn_w_down": ffn_w_down}


def reference(x, meta_tokens, pre_mix_norm, post_mix_norm, pre_ffn_norm, post_ffn_norm,
              w_in, w_out, ssd_conv_w, ssd_conv_b, ssd_dt_bias, ssd_a_log, ssd_d, ssd_norm_w,
              rwkv_mu, rwkv_w0, rwkv_w2, rwkv_a0, rwkv_a2, rwkv_g2, rwkv_k_k, rwkv_k_a,
              rwkv_r_k, rwkv_ln_w, rwkv_ln_b, lru_conv_w, lru_conv_b, lru_wa, lru_ba,
              lru_wx, lru_bx, lru_lambda, ret_gn_w, ffn_w_gate, ffn_w_up, ffn_w_down):
    Bsz = x.shape[0]
    meta = jnp.broadcast_to(meta_tokens.astype(x.dtype)[None], (Bsz, N_META, x.shape[-1]))
    h = jnp.concatenate([meta, x], axis=1)
    for l in range(DEPTH):
        hn = rms_norm(h, pre_mix_norm[l])
        proj = hn @ w_in[l]
        p_ssd, p_rwkv, p_lru, p_ret = split_last(proj, [SSD_IN, RWKV_IN, LRU_IN, RET_IN])
        y = jnp.concatenate([
            ssd_mix(p_ssd, ssd_conv_w[l], ssd_conv_b[l], ssd_dt_bias[l], ssd_a_log[l], ssd_d[l], ssd_norm_w[l]),
            rwkv7_mix(p_rwkv, rwkv_mu[l], rwkv_w0[l], rwkv_w2[l], rwkv_a0[l], rwkv_a2[l], rwkv_g2[l],
                      rwkv_k_k[l], rwkv_k_a[l], rwkv_r_k[l], rwkv_ln_w[l], rwkv_ln_b[l]),
            rglru_mix(p_lru, lru_conv_w[l], lru_conv_b[l], lru_wa[l], lru_ba[l], lru_wx[l], lru_bx[l], lru_lambda[l]),
            retention_mix(p_ret, ret_gn_w[l]),
        ], axis=-1).astype(h.dtype)
        h = h + rms_norm(y @ w_out[l], post_mix_norm[l])
        hn = rms_norm(h, pre_ffn_norm[l])
        f = (jax.nn.silu(hn @ ffn_w_gate[l]) * (hn @ ffn_w_up[l])) @ ffn_w_down[l]
        h = h + rms_norm(f, post_ffn_norm[l])
    return h[:, N_META:]
```

```python
import functools
import math

import numpy as np
import jax
import jax.numpy as jnp
from jax import lax
from jax.experimental import pallas as pl
from jax.experimental.pallas import tpu as pltpu

F32 = jnp.float32
BF16 = jnp.bfloat16

D_MODEL = 1024
N_META = 16
CHUNK = 128
PAD = CHUNK - N_META
NORM_EPS = 1e-6

SSD_HEADS = 4
SSD_HEAD_DIM = 64
SSD_WIDTH = 256
SSD_STATE = 128
SSD_CONV = 4
SSD_CONV_CH = 768
SSD_IN = 1028

RWKV_HEADS = 4
RWKV_HEAD_DIM = 64
RWKV_WIDTH = 256
RWKV_IN = 1024
RWKV_GN_EPS = 64e-5
RWKV_CHUNK = 64

LRU_WIDTH = 256
LRU_CONV = 4
LRU_C = 8.0
LRU_IN = 512

RET_HEADS = 4
RET_QK_DIM = 32
RET_V_DIM = 64
RET_WIDTH = 256
RET_QK = RET_HEADS * RET_QK_DIM
RET_IN = 768
RET_GN_EPS = 1e-5
ROPE_BASE = 10000.0

D_FF = 2816

VMEM_LIMIT_BYTES = 52 * 1024 * 1024

P_SSD_W = SSD_WIDTH + SSD_CONV_CH + SSD_WIDTH
IN_COLS = P_SSD_W + RWKV_IN + LRU_IN + RET_IN


def _mm(a, b):
    return jnp.dot(a.astype(BF16), b.astype(BF16), preferred_element_type=F32)


def _mm_nt(a, b):
    return lax.dot_general(a.astype(BF16), b.astype(BF16), (((1,), (1,)), ((), ())),
                           preferred_element_type=F32)


def _mm_tn(a, b):
    return lax.dot_general(a.astype(BF16), b.astype(BF16), (((0,), (0,)), ((), ())),
                           preferred_element_type=F32)


def _mm_exact(a, b):
    return jnp.dot(a, b, preferred_element_type=F32, precision=lax.Precision.HIGHEST)


def _sigmoid(x):
    return 1.0 / (1.0 + jnp.exp(-x))


def _silu(x):
    return x * _sigmoid(x)


def _softplus(x):
    return jnp.maximum(x, 0.0) + jnp.log1p(jnp.exp(-jnp.abs(x)))


def _rms(x, w):
    return x * lax.rsqrt(jnp.mean(x * x, axis=-1, keepdims=True) + NORM_EPS) * w


def _causal_conv(buf, w, b, rows):
    acc = b + w[3:4, :] * buf[8:8 + rows, :]
    acc = acc + w[2:3, :] * buf[7:7 + rows, :]
    acc = acc + w[1:2, :] * buf[6:6 + rows, :]
    acc = acc + w[0:1, :] * buf[5:5 + rows, :]
    return acc


def _in_proj_kernel(h_ref, nw_ref, w_ref, ssd_ref, rwkv_ref, lru_ref, ret_ref):
    tm = h_ref.shape[0]
    row = pl.program_id(1) * tm + lax.broadcasted_iota(jnp.int32, (tm, 1), 0)
    hn = _rms(h_ref[...], nw_ref[...])
    hn = jnp.where(row >= PAD, hn, 0.0).astype(BF16)
    o = 0
    for ref in (ssd_ref, rwkv_ref, lru_ref, ret_ref):
        n = ref.shape[1]
        ref[...] = jnp.dot(hn, w_ref[:, o:o + n], preferred_element_type=F32)
        o += n


def _in_proj(h, nw, w, tm):
    bsz, tp, d = h.shape
    widths = (P_SSD_W, RWKV_IN, LRU_IN, RET_IN)
    return pl.pallas_call(
        _in_proj_kernel,
        grid=(bsz, tp // tm),
        in_specs=[pl.BlockSpec((None, tm, d), lambda b, i: (b, i, 0)),
                  pl.BlockSpec((1, d), lambda b, i: (0, 0)),
                  pl.BlockSpec((d, IN_COLS), lambda b, i: (0, 0))],
        out_specs=[pl.BlockSpec((None, tm, n), lambda b, i: (b, i, 0)) for n in widths],
        out_shape=[jax.ShapeDtypeStruct((bsz, tp, n), F32) for n in widths],
        compiler_params=pltpu.CompilerParams(
            dimension_semantics=("parallel", "arbitrary"), vmem_limit_bytes=VMEM_LIMIT_BYTES),
        name="in_proj",
    )(h, nw, w)


def _ssd_kernel(p_ref, cw_ref, cb_ref, dtb_ref, alog_ref, dsk_ref, nw_ref, y_ref, xbuf, state):
    c = pl.program_id(1)
    L = CHUNK

    @pl.when(c == 0)
    def _():
        xbuf[0:8, :] = jnp.zeros((8, SSD_CONV_CH), F32)
        state[...] = jnp.zeros(state.shape, F32)

    xbuf[8:8 + L, :] = p_ref[:, SSD_WIDTH:SSD_WIDTH + SSD_CONV_CH]
    xbc = _silu(_causal_conv(xbuf, cw_ref[...], cb_ref[...], L))
    xbuf[0:8, :] = xbuf[L:L + 8, :]
    xs = xbc[:, 0:256]
    bs = xbc[:, 256:512]
    cs = xbc[:, 512:768]

    row = c * L + lax.broadcasted_iota(jnp.int32, (L, 1), 0)
    dt = _softplus(p_ref[:, SSD_WIDTH + SSD_CONV_CH:P_SSD_W] + dtb_ref[...])
    dt = jnp.where(row >= PAD, dt, 0.0)
    a = dt * (-jnp.exp(alog_ref[...]))
    ri = lax.broadcasted_iota(jnp.int32, (L, L), 0)
    ci = lax.broadcasted_iota(jnp.int32, (L, L), 1)
    causal = ri >= ci
    acs = _mm_exact(causal.astype(F32), a)
    a_last = acs[L - 1:L, :]
    xdt = xs * dt
    z = p_ref[:, 0:SSD_WIDTH]
    lane_lo = lax.broadcasted_iota(jnp.int32, (1, 128), 1) < SSD_HEAD_DIM

    for g in range(2):
        sl = slice(128 * g, 128 * g + 128)
        acs_g = acs[:, sl]
        acs_t = acs_g.T
        b_g, c_g, x_g = bs[:, sl], cs[:, sl], xdt[:, sl]
        gmat = _mm_nt(c_g, b_g)
        ms = []
        for hh in range(2):
            o = SSD_HEAD_DIM * hh
            diff = acs_g[:, o:o + 1] - acs_t[o:o + 1, :]
            ms.append(gmat * jnp.exp(jnp.where(causal, diff, -1e30)))
        mcat = jnp.concatenate(ms, axis=1)
        xbd = jnp.concatenate([jnp.where(lane_lo, x_g, 0.0), jnp.where(lane_lo, 0.0, x_g)], axis=0)
        y_diag = _mm(mcat, xbd)
        s_in = state[g]
        y_off = _mm(c_g, s_in) * jnp.exp(acs_g)
        tot = jnp.exp(a_last[:, sl])
        state[g] = s_in * tot + _mm_tn(b_g, x_g * jnp.exp(a_last[:, sl] - acs_g))
        y_g = y_diag + y_off + dsk_ref[:, sl] * xs[:, sl]
        y_g = y_g * _silu(z[:, sl])
        y_ref[:, sl] = _rms(y_g, nw_ref[:, sl])


def _ssd_mix(p, cw, cb, dtb, alog, dsk, nw):
    bsz, tp, _ = p.shape
    vec = lambda n: pl.BlockSpec((1, n), lambda b, c: (0, 0))
    return pl.pallas_call(
        _ssd_kernel,
        grid=(bsz, tp // CHUNK),
        in_specs=[pl.BlockSpec((None, CHUNK, P_SSD_W), lambda b, c: (b, c, 0)),
                  pl.BlockSpec((SSD_CONV, SSD_CONV_CH), lambda b, c: (0, 0)),
                  vec(SSD_CONV_CH), vec(SSD_WIDTH), vec(SSD_WIDTH), vec(SSD_WIDTH), vec(SSD_WIDTH)],
        out_specs=pl.BlockSpec((None, CHUNK, SSD_WIDTH), lambda b, c: (b, c, 0)),
        out_shape=jax.ShapeDtypeStruct((bsz, tp, SSD_WIDTH), F32),
        scratch_shapes=[pltpu.VMEM((CHUNK + 8, SSD_CONV_CH), F32),
                        pltpu.VMEM((2, SSD_STATE, 128), F32)],
        compiler_params=pltpu.CompilerParams(dimension_semantics=("parallel", "arbitrary")),
        name="ssd_mix",
    )(p, cw, cb, dtb, alog, dsk, nw)


def _rwkv_kernel(p_ref, mu_ref, w0_ref, a0_ref, wa2_ref, g2_ref, kk_ref, ka_ref, rk_ref,
                 lnw_ref, lnb_ref, y_ref, pbuf, state):
    c = pl.program_id(1)
    C = RWKV_CHUNK
    W = RWKV_WIDTH
    R = RWKV_HEADS * C

    @pl.when(c == 0)
    def _():
        pbuf[0:8, :] = jnp.zeros((8, RWKV_IN), F32)
        state[...] = jnp.zeros(state.shape, F32)

    pbuf[8:8 + C, :] = p_ref[...]
    p = pbuf[8:8 + C, :]
    p = p + (pbuf[7:7 + C, :] - p) * mu_ref[...]
    pbuf[0:8, :] = pbuf[C:C + 8, :]
    r, k, v = p[:, 0:W], p[:, W:2 * W], p[:, 2 * W:3 * W]
    lat = p[:, 3 * W:3 * W + 128]
    lane = lax.broadcasted_iota(jnp.int32, (1, 128), 1)
    lat = jnp.where(lane < 64, jnp.tanh(lat), lat)
    wa = _mm(lat, wa2_ref[...])
    w = -_softplus(-(w0_ref[...] + wa[:, 0:W])) - 0.5
    logw = -jnp.exp(w)
    a = _sigmoid(a0_ref[...] + wa[:, W:2 * W])
    g = _mm(_sigmoid(p[:, 3 * W + 128:3 * W + 256]), g2_ref[...])

    hr = lax.broadcasted_iota(jnp.int32, (W, W), 0) // RWKV_HEAD_DIM
    hc = lax.broadcasted_iota(jnp.int32, (W, W), 1) // RWKV_HEAD_DIM
    seg = (hr == hc).astype(F32)
    kk = k * kk_ref[...]
    kk = kk / jnp.maximum(jnp.sqrt(_mm_exact(kk * kk, seg)), 1e-12)
    k2 = k * (1.0 + (a - 1.0) * ka_ref[...])
    bonus = _mm_exact(r * k2 * rk_ref[...], seg) * v

    ti = lax.broadcasted_iota(jnp.int32, (C, C), 0)
    si = lax.broadcasted_iota(jnp.int32, (C, C), 1)
    cum = _mm_exact((ti >= si).astype(F32), logw)
    cum_prev = cum - logw
    e_neg = jnp.exp(-cum)
    at = -kk * jnp.exp(cum_prev)
    rt = r * jnp.exp(cum)
    bt = kk * a * e_neg
    kt = k2 * e_neg

    lane_head = lax.broadcasted_iota(jnp.int32, (1, W), 1) // RWKV_HEAD_DIM

    def stack(x):
        return jnp.concatenate([jnp.where(lane_head == h, x, 0.0) for h in range(RWKV_HEADS)], axis=0)

    def tile(x):
        return jnp.concatenate([x] * RWKV_HEADS, axis=0)

    rr = lax.broadcasted_iota(jnp.int32, (R, R), 0)
    cc = lax.broadcasted_iota(jnp.int32, (R, R), 1)
    same = (rr // C) == (cc // C)
    strict = same & ((cc % C) < (rr % C))
    incl = same & ((cc % C) <= (rr % C))

    at_s, rt_s = stack(at), stack(rt)
    bt_t, kt_t = tile(bt), tile(kt)
    a_ab = jnp.where(strict, _mm_nt(at_s, bt_t), 0.0)
    a_ak = jnp.where(strict, _mm_nt(at_s, kt_t), 0.0)
    a_rb = jnp.where(incl, _mm_nt(rt_s, bt_t), 0.0)
    a_rk = jnp.where(incl, _mm_nt(rt_s, kt_t), 0.0)

    npow = a_ab
    tinv = jnp.where(rr == cc, 1.0, 0.0) + npow
    n = 1
    while 2 * n < C:
        npow = _mm(npow, npow)
        tinv = tinv + _mm(tinv, npow)
        n *= 2

    same_v = (rr // C) == (cc // RWKV_HEAD_DIM)
    v_bd = jnp.where(same_v, tile(v), 0.0)
    s_in = state[...]
    u = _mm(tinv, _mm_nt(at_s, s_in) + _mm(a_ak, v_bd))
    y_bd = _mm_nt(rt_s, s_in) + _mm(a_rb, u) + _mm(a_rk, v_bd)
    y = y_bd[0:C] + y_bd[C:2 * C] + y_bd[2 * C:3 * C] + y_bd[3 * C:4 * C]
    pc = jnp.exp(cum[C - 1:C, :])
    state[...] = s_in * pc + _mm_tn(u, stack(bt * pc)) + _mm_tn(v_bd, stack(kt * pc))

    inv_n = 1.0 / RWKV_HEAD_DIM
    mu_y = _mm_exact(y, seg) * inv_n
    yc = y - mu_y
    var = _mm_exact(yc * yc, seg) * inv_n
    yn = yc * lax.rsqrt(var + RWKV_GN_EPS) * lnw_ref[...] + lnb_ref[...]
    y_ref[...] = (yn + bonus) * g


def _rwkv_mix(p, mu, w0, a0, wa2, g2, kk, ka, rk, lnw, lnb):
    bsz, tp, _ = p.shape
    C = RWKV_CHUNK
    vec = lambda n: pl.BlockSpec((1, n), lambda b, c: (0, 0))
    return pl.pallas_call(
        _rwkv_kernel,
        grid=(bsz, tp // C),
        in_specs=[pl.BlockSpec((None, C, RWKV_IN), lambda b, c: (b, c, 0)),
                  vec(RWKV_IN), vec(RWKV_WIDTH), vec(RWKV_WIDTH),
                  pl.BlockSpec((128, 2 * RWKV_WIDTH), lambda b, c: (0, 0)),
                  pl.BlockSpec((128, RWKV_WIDTH), lambda b, c: (0, 0)),
                  vec(RWKV_WIDTH), vec(RWKV_WIDTH), vec(RWKV_WIDTH), vec(RWKV_WIDTH), vec(RWKV_WIDTH)],
        out_specs=pl.BlockSpec((None, C, RWKV_WIDTH), lambda b, c: (b, c, 0)),
        out_shape=jax.ShapeDtypeStruct((bsz, tp, RWKV_WIDTH), F32),
        scratch_shapes=[pltpu.VMEM((C + 8, RWKV_IN), F32),
                        pltpu.VMEM((RWKV_WIDTH, RWKV_WIDTH), F32)],
        compiler_params=pltpu.CompilerParams(dimension_semantics=("parallel", "arbitrary")),
        name="rwkv_mix",
    )(p, mu, w0, a0, wa2, g2, kk, ka, rk, lnw, lnb)


def _lru_kernel(p_ref, cw_ref, cb_ref, wax_ref, bax_ref, lam_ref, y_ref, xbuf, hprev):
    c = pl.program_id(1)
    L = CHUNK
    W = LRU_WIDTH

    @pl.when(c == 0)
    def _():
        xbuf[0:8, :] = jnp.zeros((8, W), F32)
        hprev[...] = jnp.zeros(hprev.shape, F32)

    xbuf[8:8 + L, :] = p_ref[:, 0:W]
    xc = _causal_conv(xbuf, cw_ref[...], cb_ref[...], L)
    xbuf[0:8, :] = xbuf[L:L + 8, :]
    gates = _sigmoid(_mm(xc, wax_ref[...]) + bax_ref[...])
    r, i = gates[:, 0:W], gates[:, W:2 * W]
    log_a = -LRU_C * r * _softplus(-lam_ref[...])
    a = jnp.exp(log_a)
    t = jnp.tanh(log_a)
    one_minus_a2 = -2.0 * t / (1.0 - t)
    u = jnp.sqrt(one_minus_a2) * (i * xc)
    rowi = lax.broadcasted_iota(jnp.int32, (L, 1), 0)
    u = jnp.where(c * L + rowi >= PAD, u, 0.0)

    s = 1
    while s < L:
        keep = rowi >= s
        a_sh = jnp.where(keep, pltpu.roll(a, s, axis=0), 1.0)
        u_sh = jnp.where(keep, pltpu.roll(u, s, axis=0), 0.0)
        u = u + a * u_sh
        a = a * a_sh
        s *= 2
    h = u + a * hprev[0:1, :]
    hprev[0:1, :] = h[L - 1:L, :]
    y_ref[...] = h * jax.nn.gelu(p_ref[:, W:2 * W], approximate=True)


def _lru_mix(p, cw, cb, wax, bax, lam):
    bsz, tp, _ = p.shape
    vec = lambda n: pl.BlockSpec((1, n), lambda b, c: (0, 0))
    return pl.pallas_call(
        _lru_kernel,
        grid=(bsz, tp // CHUNK),
        in_specs=[pl.BlockSpec((None, CHUNK, LRU_IN), lambda b, c: (b, c, 0)),
                  pl.BlockSpec((LRU_CONV, LRU_WIDTH), lambda b, c: (0, 0)),
                  vec(LRU_WIDTH),
                  pl.BlockSpec((LRU_WIDTH, 2 * LRU_WIDTH), lambda b, c: (0, 0)),
                  vec(2 * LRU_WIDTH), vec(LRU_WIDTH)],
        out_specs=pl.BlockSpec((None, CHUNK, LRU_WIDTH), lambda b, c: (b, c, 0)),
        out_shape=jax.ShapeDtypeStruct((bsz, tp, LRU_WIDTH), F32),
        scratch_shapes=[pltpu.VMEM((CHUNK + 8, LRU_WIDTH), F32),
                        pltpu.VMEM((8, LRU_WIDTH), F32)],
        compiler_params=pltpu.CompilerParams(dimension_semantics=("parallel", "arbitrary")),
        name="lru_mix",
    )(p, cw, cb, wax, bax, lam)


def _ret_kernel(p_ref, cos_ref, sin_ref, dmat_ref, kdec_ref, qdec_ref, cross_ref, gnw_ref,
                y_ref, state):
    c = pl.program_id(1)
    L = CHUNK
    W = RET_WIDTH

    @pl.when(c == 0)
    def _():
        state[...] = jnp.zeros(state.shape, F32)

    lane = lax.broadcasted_iota(jnp.int32, (1, RET_QK), 1)
    first_half = (lane % RET_QK_DIM) < (RET_QK_DIM // 2)
    half = RET_QK_DIM // 2

    def rope(x):
        swapped = jnp.where(first_half, pltpu.roll(x, RET_QK - half, axis=1), pltpu.roll(x, half, axis=1))
        return x * cos_ref[...] + swapped * sin_ref[...]

    q = rope(p_ref[:, 0:RET_QK])
    k = rope(p_ref[:, RET_QK:2 * RET_QK]) * (RET_QK_DIM ** -0.5)
    v = p_ref[:, 2 * RET_QK:2 * RET_QK + W]
    g = p_ref[:, 2 * RET_QK + W:2 * RET_QK + 2 * W]

    qk_head = lane // RET_QK_DIM
    v_head = lax.broadcasted_iota(jnp.int32, (1, W), 1) // RET_V_DIM
    y = _mm(q * qdec_ref[...], state[...])
    for h in range(RET_HEADS):
        s_h = _mm_nt(jnp.where(qk_head == h, q, 0.0), k) * dmat_ref[h]
        y = y + _mm(s_h, jnp.where(v_head == h, v, 0.0))
    rh = lax.broadcasted_iota(jnp.int32, (RET_QK, W), 0) // RET_QK_DIM
    ch = lax.broadcasted_iota(jnp.int32, (RET_QK, W), 1) // RET_V_DIM
    kv = jnp.where(rh == ch, _mm_tn(k * kdec_ref[...], v), 0.0)
    state[...] = state[...] * cross_ref[...] + kv

    hr = lax.broadcasted_iota(jnp.int32, (W, W), 0) // RET_V_DIM
    hc = lax.broadcasted_iota(jnp.int32, (W, W), 1) // RET_V_DIM
    seg = (hr == hc).astype(F32)
    inv_n = 1.0 / RET_V_DIM
    yc = y - _mm_exact(y, seg) * inv_n
    var = _mm_exact(yc * yc, seg) * inv_n
    y_ref[...] = yc * lax.rsqrt(var + RET_GN_EPS) * gnw_ref[...] * _silu(g)


def _ret_mix(p, cos_t, sin_t, dmat, kdec, qdec, cross, gnw):
    bsz, tp, _ = p.shape
    const2 = lambda s: pl.BlockSpec(s, lambda b, c: (0, 0))
    return pl.pallas_call(
        _ret_kernel,
        grid=(bsz, tp // CHUNK),
        in_specs=[pl.BlockSpec((None, CHUNK, RET_IN), lambda b, c: (b, c, 0)),
                  pl.BlockSpec((CHUNK, RET_QK), lambda b, c: (c, 0)),
                  pl.BlockSpec((CHUNK, RET_QK), lambda b, c: (c, 0)),
                  pl.BlockSpec((RET_HEADS, CHUNK, CHUNK), lambda b, c: (0, 0, 0)),
                  const2((CHUNK, RET_QK)), const2((CHUNK, RET_QK)),
                  const2((RET_QK, RET_WIDTH)), const2((1, RET_WIDTH))],
        out_specs=pl.BlockSpec((None, CHUNK, RET_WIDTH), lambda b, c: (b, c, 0)),
        out_shape=jax.ShapeDtypeStruct((bsz, tp, RET_WIDTH), F32),
        scratch_shapes=[pltpu.VMEM((RET_QK, RET_WIDTH), F32)],
        compiler_params=pltpu.CompilerParams(dimension_semantics=("parallel", "arbitrary")),
        name="ret_mix",
    )(p, cos_t, sin_t, dmat, kdec, qdec, cross, gnw)


def _retention_tables(tp):
    half = RET_QK_DIM // 2
    lane = np.arange(RET_QK)
    freqs = ROPE_BASE ** (-np.arange(half, dtype=np.float64) / half)
    pos = np.arange(tp, dtype=np.float64) - PAD
    ang = pos[:, None] * freqs[lane % half][None, :]
    sign = np.where((lane % RET_QK_DIM) < half, -1.0, 1.0)
    cos_t = np.cos(ang)
    sin_t = np.sin(ang) * sign[None, :]
    log_g = np.log1p(-np.exp2(-5.0 - np.arange(RET_HEADS, dtype=np.float64)))
    idx = np.arange(CHUNK)
    rel = idx[:, None] - idx[None, :]
    dmat = np.where(rel >= 0, np.exp(np.maximum(rel, 0)[None] * log_g[:, None, None]), 0.0)
    lg_lane = log_g[lane // RET_QK_DIM]
    kdec = np.exp((CHUNK - 1 - idx)[:, None] * lg_lane[None, :])
    qdec = np.exp((idx + 1)[:, None] * lg_lane[None, :])
    cross = np.broadcast_to(np.exp(CHUNK * lg_lane)[:, None], (RET_QK, RET_WIDTH))
    f = lambda x: jnp.asarray(np.ascontiguousarray(x), F32)
    return f(cos_t), f(sin_t), f(dmat), f(kdec), f(qdec), f(cross)


def _out_ffn_kernel(h_ref, y0_ref, y1_ref, y2_ref, y3_ref, wo_ref, n1_ref, n2_ref, n3_ref,
                    wg_ref, wu_ref, wd_ref, o_ref, h1_s, hn_s, acc_s):
    j = pl.program_id(1)

    @pl.when(j == 0)
    def _():
        u = None
        for i, yr in enumerate((y0_ref, y1_ref, y2_ref, y3_ref)):
            part = _mm(yr[...], wo_ref[256 * i:256 * i + 256, :])
            u = part if u is None else u + part
        h1 = h_ref[...] + _rms(u, n1_ref[...])
        h1_s[...] = h1
        hn_s[...] = _rms(h1, n2_ref[...]).astype(BF16)
        acc_s[...] = jnp.zeros(acc_s.shape, F32)

    hn = hn_s[...]
    act = _silu(jnp.dot(hn, wg_ref[...], preferred_element_type=F32)) * \
        jnp.dot(hn, wu_ref[...], preferred_element_type=F32)
    acc_s[...] += jnp.dot(act.astype(BF16), wd_ref[...], preferred_element_type=F32)

    @pl.when(j == pl.num_programs(1) - 1)
    def _():
        o_ref[...] = h1_s[...] + _rms(acc_s[...], n3_ref[...])


def _out_ffn(h, ys, wo, n1, n2, n3, wg, wu, wd, tm, tf):
    m, d = h.shape
    row = lambda n: pl.BlockSpec((tm, n), lambda i, j: (i, 0))
    vec = pl.BlockSpec((1, d), lambda i, j: (0, 0))
    return pl.pallas_call(
        _out_ffn_kernel,
        grid=(m // tm, D_FF // tf),
        in_specs=[row(d), row(256), row(256), row(256), row(256),
                  pl.BlockSpec((d, d), lambda i, j: (0, 0)), vec, vec, vec,
                  pl.BlockSpec((d, tf), lambda i, j: (0, j)),
                  pl.BlockSpec((d, tf), lambda i, j: (0, j)),
                  pl.BlockSpec((tf, d), lambda i, j: (j, 0))],
        out_specs=row(d),
        out_shape=jax.ShapeDtypeStruct((m, d), F32),
        scratch_shapes=[pltpu.VMEM((tm, d), F32), pltpu.VMEM((tm, d), BF16), pltpu.VMEM((tm, d), F32)],
        compiler_params=pltpu.CompilerParams(
            dimension_semantics=("parallel", "arbitrary"), vmem_limit_bytes=VMEM_LIMIT_BYTES),
        name="out_ffn",
    )(h, *ys, wo, n1, n2, n3, wg, wu, wd)


def _block_diag(blocks):
    g, n, m = blocks.shape
    eye = jnp.eye(g, dtype=blocks.dtype)
    return (eye[:, None, :, None] * blocks[:, :, None, :]).reshape(g * n, g * m)


def _row_tile(tp):
    best = 8
    for t in range(8, 641, 8):
        if tp % t == 0:
            best = t
    return best


def kernel(x, meta_tokens, pre_mix_norm, post_mix_norm, pre_ffn_norm, post_ffn_norm, w_in, w_out, ssd_conv_w, ssd_conv_b, ssd_dt_bias, ssd_a_log, ssd_d, ssd_norm_w, rwkv_mu, rwkv_w0, rwkv_w2, rwkv_a0, rwkv_a2, rwkv_g2, rwkv_k_k, rwkv_k_a, rwkv_r_k, rwkv_ln_w, rwkv_ln_b, lru_conv_w, lru_conv_b, lru_wa, lru_ba, lru_wx, lru_bx, lru_lambda, ret_gn_w, ffn_w_gate, ffn_w_up, ffn_w_down):
    bsz, seq, d = x.shape
    depth = w_in.shape[0]
    t = N_META + seq
    tp = t + PAD
    assert d == D_MODEL and tp % CHUNK == 0
    meta = jnp.broadcast_to(meta_tokens.astype(x.dtype)[None], (bsz, N_META, d))
    h = jnp.concatenate([jnp.zeros((bsz, PAD, d), x.dtype), meta, x], axis=1)

    tm_in = _row_tile(tp)
    m_rows = bsz * tp
    tm_ffn = 1024 if m_rows % 1024 == 0 else CHUNK
    tf = 256
    tables = _retention_tables(tp)
    per_head = lambda v: jnp.repeat(v, SSD_HEAD_DIM, axis=-1)[None]
    r2 = lambda v: v[None]

    for l in range(depth):
        wl = w_in[l]
        o_r = SSD_IN
        o_l = o_r + RWKV_IN
        o_t = o_l + LRU_IN
        w_cat = jnp.concatenate([
            wl[:, 0:SSD_WIDTH + SSD_CONV_CH],
            jnp.repeat(wl[:, SSD_WIDTH + SSD_CONV_CH:SSD_IN], SSD_HEAD_DIM, axis=1),
            wl[:, o_r:o_l], wl[:, o_l:o_t], wl[:, o_t:]], axis=1).astype(BF16)
        p_ssd, p_rwkv, p_lru, p_ret = _in_proj(h, r2(pre_mix_norm[l]), w_cat, tm_in)

        y_ssd = _ssd_mix(p_ssd, ssd_conv_w[l], r2(ssd_conv_b[l]), per_head(ssd_dt_bias[l]),
                         per_head(ssd_a_log[l]), per_head(ssd_d[l]), r2(ssd_norm_w[l]))

        zero = jnp.zeros((64, RWKV_WIDTH), F32)
        wa2 = jnp.concatenate([jnp.concatenate([rwkv_w2[l], zero], axis=1),
                               jnp.concatenate([zero, rwkv_a2[l]], axis=1)], axis=0).astype(BF16)
        y_rwkv = _rwkv_mix(p_rwkv, r2(rwkv_mu[l]), r2(rwkv_w0[l]), r2(rwkv_a0[l]), wa2,
                           rwkv_g2[l].astype(BF16), r2(rwkv_k_k[l]), r2(rwkv_k_a[l]),
                           rwkv_r_k[l].reshape(1, RWKV_WIDTH), r2(rwkv_ln_w[l]), r2(rwkv_ln_b[l]))

        wax = jnp.concatenate([_block_diag(lru_wa[l]), _block_diag(lru_wx[l])], axis=1).astype(BF16)
        bax = jnp.concatenate([lru_ba[l], lru_bx[l]])[None]
        y_lru = _lru_mix(p_lru, lru_conv_w[l], r2(lru_conv_b[l]), wax, bax, r2(lru_lambda[l]))

        y_ret = _ret_mix(p_ret, *tables, r2(ret_gn_w[l]))

        flat = lambda a: a.reshape(m_rows, a.shape[-1])
        h = _out_ffn(flat(h), [flat(y_ssd), flat(y_rwkv), flat(y_lru), flat(y_ret)],
                     w_out[l].astype(BF16), r2(post_mix_norm[l]), r2(pre_ffn_norm[l]),
                     r2(post_ffn_norm[l]), ffn_w_gate[l].astype(BF16), ffn_w_up[l].astype(BF16),
                     ffn_w_down[l].astype(BF16), tm_ffn, tf).reshape(bsz, tp, d)
    return h[:, PAD + N_META:]
```

```python
import functools
import math

import numpy as np
import jax
import jax.numpy as jnp
from jax import lax
from jax.experimental import pallas as pl
from jax.experimental.pallas import tpu as pltpu

F32 = jnp.float32
BF16 = jnp.bfloat16

D_MODEL = 1024
N_META = 16
CHUNK = 128
PAD = CHUNK - N_META
NORM_EPS = 1e-6

SSD_HEADS = 4
SSD_HEAD_DIM = 64
SSD_WIDTH = 256
SSD_STATE = 128
SSD_CONV = 4
SSD_CONV_CH = 768
SSD_IN = 1028

RWKV_HEADS = 4
RWKV_HEAD_DIM = 64
RWKV_WIDTH = 256
RWKV_IN = 1024
RWKV_GN_EPS = 64e-5
RWKV_CHUNK = 64

LRU_WIDTH = 256
LRU_CONV = 4
LRU_C = 8.0
LRU_IN = 512

RET_HEADS = 4
RET_QK_DIM = 32
RET_V_DIM = 64
RET_WIDTH = 256
RET_QK = RET_HEADS * RET_QK_DIM
RET_IN = 768
RET_GN_EPS = 1e-5
ROPE_BASE = 10000.0

D_FF = 2816

VMEM_LIMIT_BYTES = 52 * 1024 * 1024

P_SSD_W = SSD_WIDTH + SSD_CONV_CH + SSD_WIDTH
IN_COLS = P_SSD_W + RWKV_IN + LRU_IN + RET_IN


def _mm(a, b):
    return jnp.dot(a.astype(BF16), b.astype(BF16), preferred_element_type=F32)


def _mm_nt(a, b):
    return lax.dot_general(a.astype(BF16), b.astype(BF16), (((1,), (1,)), ((), ())),
                           preferred_element_type=F32)


def _mm_tn(a, b):
    return lax.dot_general(a.astype(BF16), b.astype(BF16), (((0,), (0,)), ((), ())),
                           preferred_element_type=F32)


def _mm_exact(a, b):
    return jnp.dot(a, b, preferred_element_type=F32, precision=lax.Precision.HIGHEST)


def _sigmoid(x):
    return 1.0 / (1.0 + jnp.exp(-x))


def _silu(x):
    return x * _sigmoid(x)


def _softplus(x):
    return jnp.maximum(x, 0.0) + jnp.log1p(jnp.exp(-jnp.abs(x)))


def _rms(x, w):
    return x * lax.rsqrt(jnp.mean(x * x, axis=-1, keepdims=True) + NORM_EPS) * w


def _causal_conv(buf, w, b, rows):
    acc = b + w[3:4, :] * buf[8:8 + rows, :]
    acc = acc + w[2:3, :] * buf[7:7 + rows, :]
    acc = acc + w[1:2, :] * buf[6:6 + rows, :]
    acc = acc + w[0:1, :] * buf[5:5 + rows, :]
    return acc


def _prep_w_in_kernel(w_ref, o_ref):
    main = SSD_WIDTH + SSD_CONV_CH
    o_ref[:, 0:main] = w_ref[:, 0:main].astype(BF16)
    lane_lo = lax.broadcasted_iota(jnp.int32, (1, 128), 1) < SSD_HEAD_DIM
    for g in range(SSD_HEADS // 2):
        c0 = w_ref[:, main + 2 * g:main + 2 * g + 1]
        c1 = w_ref[:, main + 2 * g + 1:main + 2 * g + 2]
        o_ref[:, main + 128 * g:main + 128 * g + 128] = jnp.where(lane_lo, c0, c1).astype(BF16)
    o_ref[:, P_SSD_W:] = w_ref[:, SSD_IN:].astype(BF16)


def _prep_w_in(w_in):
    depth, d, n = w_in.shape
    tr = 256
    return pl.pallas_call(
        _prep_w_in_kernel,
        grid=(depth, d // tr),
        in_specs=[pl.BlockSpec((None, tr, n), lambda l, i: (l, i, 0))],
        out_specs=pl.BlockSpec((None, tr, IN_COLS), lambda l, i: (l, i, 0)),
        out_shape=jax.ShapeDtypeStruct((depth, d, IN_COLS), BF16),
        compiler_params=pltpu.CompilerParams(dimension_semantics=("parallel", "parallel")),
        name="prep_w_in",
    )(w_in)


def _in_proj_kernel(h_ref, nw_ref, w_ref, ssd_ref, rwkv_ref, lru_ref, ret_ref):
    tm = h_ref.shape[0]
    row = pl.program_id(1) * tm + lax.broadcasted_iota(jnp.int32, (tm, 1), 0)
    hn = _rms(h_ref[...], nw_ref[...])
    hn = jnp.where(row >= PAD, hn, 0.0).astype(BF16)
    o = 0
    for ref in (ssd_ref, rwkv_ref, lru_ref, ret_ref):
        n = ref.shape[1]
        ref[...] = jnp.dot(hn, w_ref[:, o:o + n], preferred_element_type=F32)
        o += n


def _in_proj(h, nw, w, tm):
    bsz, tp, d = h.shape
    widths = (P_SSD_W, RWKV_IN, LRU_IN, RET_IN)
    return pl.pallas_call(
        _in_proj_kernel,
        grid=(bsz, tp // tm),
        in_specs=[pl.BlockSpec((None, tm, d), lambda b, i: (b, i, 0)),
                  pl.BlockSpec((1, d), lambda b, i: (0, 0)),
                  pl.BlockSpec((d, IN_COLS), lambda b, i: (0, 0))],
        out_specs=[pl.BlockSpec((None, tm, n), lambda b, i: (b, i, 0)) for n in widths],
        out_shape=[jax.ShapeDtypeStruct((bsz, tp, n), F32) for n in widths],
        compiler_params=pltpu.CompilerParams(
            dimension_semantics=("parallel", "arbitrary"), vmem_limit_bytes=VMEM_LIMIT_BYTES),
        name="in_proj",
    )(h, nw, w)


def _split_bf16(x, parts):
    out = []
    for _ in range(parts):
        hi = x.astype(BF16)
        out.append(hi)
        x = x - hi.astype(F32)
    return out


def _cumsum_rows(x, tril3_ref):
    return jnp.dot(tril3_ref[...], jnp.concatenate(_split_bf16(x, 3), axis=0), preferred_element_type=F32)


def _tril3(n):
    tril = (np.arange(n)[:, None] >= np.arange(n)[None, :]).astype(np.float32)
    return jnp.asarray(np.concatenate([tril] * 3, axis=1), BF16)


def _ssd_kernel(p_ref, cw_ref, cb_ref, dtb_ref, alog_ref, dsk_ref, nw_ref, tril3_ref, y_ref, xbuf, state):
    c = pl.program_id(1)
    nb = p_ref.shape[0]
    L = CHUNK

    @pl.when(c == 0)
    def _():
        xbuf[:, 0:8, :] = jnp.zeros((nb, 8, SSD_CONV_CH), F32)
        state[...] = jnp.zeros(state.shape, F32)

    row = c * L + lax.broadcasted_iota(jnp.int32, (L, 1), 0)
    causal = lax.broadcasted_iota(jnp.int32, (L, L), 0) >= lax.broadcasted_iota(jnp.int32, (L, L), 1)
    lane_lo = lax.broadcasted_iota(jnp.int32, (1, 128), 1) < SSD_HEAD_DIM
    groups = [slice(128 * g, 128 * g + 128) for g in range(2)]

    def prep(b):
        buf = xbuf.at[b]
        buf[8:8 + L, :] = p_ref[b, :, SSD_WIDTH:SSD_WIDTH + SSD_CONV_CH]
        xbc = _silu(_causal_conv(buf, cw_ref[...], cb_ref[...], L))
        buf[0:8, :] = buf[L:L + 8, :]
        dt = _softplus(p_ref[b, :, SSD_WIDTH + SSD_CONV_CH:P_SSD_W] + dtb_ref[...])
        dt = jnp.where(row >= PAD, dt, 0.0)
        a = dt * (-jnp.exp(alog_ref[...]))
        xs = xbc[:, 0:256]
        return dict(xs=xs, bs=xbc[:, 256:512], cs=xbc[:, 512:768], xdt=xs * dt,
                    acs=_cumsum_rows(a, tril3_ref), y=[None, None])

    def gram(d, g):
        sl = groups[g]
        return _mm_nt(d["cs"][:, sl], d["bs"][:, sl])

    def chunk(b, d, g, gmat):
        sl = groups[g]
        acs_g = d["acs"][:, sl]
        acs_t = acs_g.T
        a_last = acs_g[L - 1:L, :]
        b_g, c_g, x_g = d["bs"][:, sl], d["cs"][:, sl], d["xdt"][:, sl]
        ms = []
        for hh in range(2):
            o = SSD_HEAD_DIM * hh
            diff = acs_g[:, o:o + 1] - acs_t[o:o + 1, :]
            ms.append(gmat * jnp.exp(jnp.where(causal, diff, -1e30)))
        mcat = jnp.concatenate(ms, axis=1)
        xbd = jnp.concatenate([jnp.where(lane_lo, x_g, 0.0), jnp.where(lane_lo, 0.0, x_g)], axis=0)
        s_in = state[b, g]
        d["y"][g] = _mm(mcat, xbd) + _mm(c_g, s_in) * jnp.exp(acs_g)
        state[b, g] = s_in * jnp.exp(a_last) + _mm_tn(b_g, x_g * jnp.exp(a_last - acs_g))

    def finish(b, d, g):
        sl = groups[g]
        y_g = (d["y"][g] + dsk_ref[:, sl] * d["xs"][:, sl]) * _silu(p_ref[b, :, sl])
        y_ref[b, :, sl] = _rms(y_g, nw_ref[:, sl])

    ds = [prep(b) for b in range(nb)]
    gm = [[gram(d, g) for g in range(2)] for d in ds]
    for b, d in enumerate(ds):
        for g in range(2):
            chunk(b, d, g, gm[b][g])
    for b, d in enumerate(ds):
        for g in range(2):
            finish(b, d, g)


def _ssd_mix(p, cw, cb, dtb, alog, dsk, nw, nb):
    bsz, tp, _ = p.shape
    vec = lambda n: pl.BlockSpec((1, n), lambda b, c: (0, 0))
    return pl.pallas_call(
        _ssd_kernel,
        grid=(bsz // nb, tp // CHUNK),
        in_specs=[pl.BlockSpec((nb, CHUNK, P_SSD_W), lambda b, c: (b, c, 0)),
                  pl.BlockSpec((SSD_CONV, SSD_CONV_CH), lambda b, c: (0, 0)),
                  vec(SSD_CONV_CH), vec(SSD_WIDTH), vec(SSD_WIDTH), vec(SSD_WIDTH), vec(SSD_WIDTH),
                  pl.BlockSpec((CHUNK, 3 * CHUNK), lambda b, c: (0, 0))],
        out_specs=pl.BlockSpec((nb, CHUNK, SSD_WIDTH), lambda b, c: (b, c, 0)),
        out_shape=jax.ShapeDtypeStruct((bsz, tp, SSD_WIDTH), F32),
        scratch_shapes=[pltpu.VMEM((nb, CHUNK + 8, SSD_CONV_CH), F32),
                        pltpu.VMEM((nb, 2, SSD_STATE, 128), F32)],
        compiler_params=pltpu.CompilerParams(dimension_semantics=("parallel", "arbitrary")),
        name="ssd_mix",
    )(p, cw, cb, dtb, alog, dsk, nw, _tril3(CHUNK))


def _seg_sum(x, seg2_ref):
    return jnp.dot(jnp.concatenate(_split_bf16(x, 2), axis=1), seg2_ref[...], preferred_element_type=F32)


def _rwkv_kernel(p_ref, mu_ref, w0_ref, a0_ref, wa2_ref, g2_ref, kk_ref, ka_ref, rk_ref,
                 lnw_ref, lnb_ref, seg2_ref, tril3_ref, mask_ref, y_ref, pbuf, state):
    c = pl.program_id(1)
    nb = p_ref.shape[0]
    C = RWKV_CHUNK
    W = RWKV_WIDTH
    R = RWKV_HEADS * C

    @pl.when(c == 0)
    def _():
        pbuf[:, 0:8, :] = jnp.zeros((nb, 8, RWKV_IN), F32)
        state[...] = jnp.zeros(state.shape, F32)

    lane = lax.broadcasted_iota(jnp.int32, (1, 128), 1)
    lane_head = lax.broadcasted_iota(jnp.int32, (1, W), 1) // RWKV_HEAD_DIM

    def stack(x):
        return jnp.concatenate([jnp.where(lane_head == h, x, 0.0) for h in range(RWKV_HEADS)], axis=0)

    def tile(x):
        return jnp.concatenate([x] * RWKV_HEADS, axis=0)

    def prep(b):
        pbuf[b, 8:8 + C, :] = p_ref[b]
        p = pbuf[b, 8:8 + C, :]
        p = p + (pbuf[b, 7:7 + C, :] - p) * mu_ref[...]
        pbuf[b, 0:8, :] = pbuf[b, C:C + 8, :]
        r, k, v = p[:, 0:W], p[:, W:2 * W], p[:, 2 * W:3 * W]
        lat = p[:, 3 * W:3 * W + 128]
        lat = jnp.where(lane < 64, jnp.tanh(lat), lat)
        wa = _mm(lat, wa2_ref[...])
        w = -_softplus(-(w0_ref[...] + wa[:, 0:W])) - 0.5
        logw = -jnp.exp(w)
        a = _sigmoid(a0_ref[...] + wa[:, W:2 * W])
        g = _mm(_sigmoid(p[:, 3 * W + 128:3 * W + 256]), g2_ref[...])
        kk = k * kk_ref[...]
        kk = kk / jnp.maximum(jnp.sqrt(_seg_sum(kk * kk, seg2_ref)), 1e-12)
        k2 = k * (1.0 + (a - 1.0) * ka_ref[...])
        bonus = _seg_sum(r * k2 * rk_ref[...], seg2_ref) * v
        cum = _cumsum_rows(logw, tril3_ref)
        e_neg = jnp.exp(-cum)
        pc = jnp.exp(cum[C - 1:C, :])
        bt = kk * a * e_neg
        kt = k2 * e_neg
        return dict(
            lhs=jnp.concatenate([stack(-kk * jnp.exp(cum - logw)), stack(r * jnp.exp(cum))], axis=0).astype(BF16),
            rhs=jnp.concatenate([tile(bt), tile(kt)], axis=0).astype(BF16),
            upd=jnp.concatenate([stack(bt * pc), stack(kt * pc)], axis=0).astype(BF16),
            v_bd=(tile(v) * mask_ref[2]).astype(BF16), pc=pc, bonus=bonus, g=g)

    def scores(d):
        strict, incl = mask_ref[0], mask_ref[1]
        sc = lax.dot_general(d["lhs"], d["rhs"], (((1,), (1,)), ((), ())), preferred_element_type=F32)
        d["npow"] = sc[0:R, 0:R] * strict
        d["a_k"] = jnp.concatenate([sc[0:R, R:2 * R] * strict, sc[R:2 * R, R:2 * R] * incl],
                                   axis=0).astype(BF16)
        d["a_rb"] = (sc[R:2 * R, 0:R] * incl).astype(BF16)
        d["tinv"] = (incl - strict) + d["npow"]

    def double(d):
        n16 = d["npow"].astype(BF16)
        d["npow"] = jnp.dot(n16, n16, preferred_element_type=F32)
        d["tinv"] = d["tinv"] + _mm(d["tinv"], d["npow"])

    def read_state(b, d):
        d["sxav"] = _mm_nt(d["lhs"], state[b]) + jnp.dot(d["a_k"], d["v_bd"], preferred_element_type=F32)

    def correction(d):
        d["u"] = _mm(d["tinv"], d["sxav"][0:R]).astype(BF16)

    def write_state(b, d):
        y_bd = d["sxav"][R:2 * R] + jnp.dot(d["a_rb"], d["u"], preferred_element_type=F32)
        d["y"] = y_bd[0:C] + y_bd[C:2 * C] + y_bd[2 * C:3 * C] + y_bd[3 * C:4 * C]
        state[b] = state[b] * d["pc"] + _mm_tn(jnp.concatenate([d["u"], d["v_bd"]], axis=0), d["upd"])

    def finish(b, d):
        inv_n = 1.0 / RWKV_HEAD_DIM
        yc = d["y"] - _seg_sum(d["y"], seg2_ref) * inv_n
        var = _seg_sum(yc * yc, seg2_ref) * inv_n
        yn = yc * lax.rsqrt(var + RWKV_GN_EPS) * lnw_ref[...] + lnb_ref[...]
        y_ref[b] = (yn + d["bonus"]) * d["g"]

    ds = [prep(b) for b in range(nb)]
    for d in ds:
        scores(d)
    n = 1
    while 2 * n < C:
        for d in ds:
            double(d)
        n *= 2
    for b, d in enumerate(ds):
        read_state(b, d)
    for d in ds:
        correction(d)
    for b, d in enumerate(ds):
        write_state(b, d)
    for b, d in enumerate(ds):
        finish(b, d)


def _rwkv_tables():
    C, W, R = RWKV_CHUNK, RWKV_WIDTH, RWKV_HEADS * RWKV_CHUNK
    head = np.arange(W) // RWKV_HEAD_DIM
    seg = (head[:, None] == head[None, :]).astype(np.float32)
    rr, cc = np.arange(R)[:, None], np.arange(R)[None, :]
    same = (rr // C) == (cc // C)
    strict = same & ((cc % C) < (rr % C))
    incl = same & ((cc % C) <= (rr % C))
    bdv = (rr // C) == (np.arange(W)[None, :] // RWKV_HEAD_DIM)
    return (jnp.asarray(np.concatenate([seg, seg], axis=0), BF16), _tril3(C),
            jnp.asarray(np.stack([strict, incl, bdv]).astype(np.float32), F32))


def _rwkv_mix(p, mu, w0, a0, wa2, g2, kk, ka, rk, lnw, lnb, nb):
    bsz, tp, _ = p.shape
    C, W, R = RWKV_CHUNK, RWKV_WIDTH, RWKV_HEADS * RWKV_CHUNK
    vec = lambda n: pl.BlockSpec((1, n), lambda b, c: (0, 0))
    full = lambda s: pl.BlockSpec(s, lambda b, c: (0,) * len(s))
    return pl.pallas_call(
        _rwkv_kernel,
        grid=(bsz // nb, tp // C),
        in_specs=[pl.BlockSpec((nb, C, RWKV_IN), lambda b, c: (b, c, 0)),
                  vec(RWKV_IN), vec(W), vec(W), full((128, 2 * W)), full((128, W)),
                  vec(W), vec(W), vec(W), vec(W), vec(W),
                  full((2 * W, W)), full((C, 3 * C)), full((3, R, R))],
        out_specs=pl.BlockSpec((nb, C, W), lambda b, c: (b, c, 0)),
        out_shape=jax.ShapeDtypeStruct((bsz, tp, W), F32),
        scratch_shapes=[pltpu.VMEM((nb, C + 8, RWKV_IN), F32),
                        pltpu.VMEM((nb, W, W), F32)],
        compiler_params=pltpu.CompilerParams(dimension_semantics=("parallel", "arbitrary")),
        name="rwkv_mix",
    )(p, mu, w0, a0, wa2, g2, kk, ka, rk, lnw, lnb, *_rwkv_tables())


def _lru_kernel(p_ref, cw_ref, cb_ref, wax_ref, bax_ref, lam_ref, y_ref, xbuf, hprev):
    c = pl.program_id(1)
    nb = p_ref.shape[0]
    L = CHUNK
    W = LRU_WIDTH

    @pl.when(c == 0)
    def _():
        xbuf[:, 0:8, :] = jnp.zeros((nb, 8, W), F32)
        hprev[...] = jnp.zeros(hprev.shape, F32)

    rowi = lax.broadcasted_iota(jnp.int32, (L, 1), 0)
    log_a_unit = -LRU_C * _softplus(-lam_ref[...])
    for b in range(nb):
        buf = xbuf.at[b]
        buf[8:8 + L, :] = p_ref[b, :, 0:W]
        xc = _causal_conv(buf, cw_ref[...], cb_ref[...], L)
        buf[0:8, :] = buf[L:L + 8, :]
        gates = _sigmoid(_mm(xc, wax_ref[...]) + bax_ref[...])
        r, i = gates[:, 0:W], gates[:, W:2 * W]
        log_a = r * log_a_unit
        a = jnp.exp(log_a)
        t = jnp.tanh(log_a)
        one_minus_a2 = -2.0 * t / (1.0 - t)
        u = jnp.sqrt(one_minus_a2) * (i * xc)
        u = jnp.where(c * L + rowi >= PAD, u, 0.0)

        s = 1
        while s < L:
            keep = rowi >= s
            a_sh = jnp.where(keep, pltpu.roll(a, s, axis=0), 1.0)
            u_sh = jnp.where(keep, pltpu.roll(u, s, axis=0), 0.0)
            u = u + a * u_sh
            a = a * a_sh
            s *= 2
        h = u + a * hprev[b, 0:1, :]
        hprev[b, 0:1, :] = h[L - 1:L, :]
        y_ref[b] = h * jax.nn.gelu(p_ref[b, :, W:2 * W], approximate=True)


def _lru_mix(p, cw, cb, wax, bax, lam, nb):
    bsz, tp, _ = p.shape
    vec = lambda n: pl.BlockSpec((1, n), lambda b, c: (0, 0))
    return pl.pallas_call(
        _lru_kernel,
        grid=(bsz // nb, tp // CHUNK),
        in_specs=[pl.BlockSpec((nb, CHUNK, LRU_IN), lambda b, c: (b, c, 0)),
                  pl.BlockSpec((LRU_CONV, LRU_WIDTH), lambda b, c: (0, 0)),
                  vec(LRU_WIDTH),
                  pl.BlockSpec((LRU_WIDTH, 2 * LRU_WIDTH), lambda b, c: (0, 0)),
                  vec(2 * LRU_WIDTH), vec(LRU_WIDTH)],
        out_specs=pl.BlockSpec((nb, CHUNK, LRU_WIDTH), lambda b, c: (b, c, 0)),
        out_shape=jax.ShapeDtypeStruct((bsz, tp, LRU_WIDTH), F32),
        scratch_shapes=[pltpu.VMEM((nb, CHUNK + 8, LRU_WIDTH), F32),
                        pltpu.VMEM((nb, 8, LRU_WIDTH), F32)],
        compiler_params=pltpu.CompilerParams(dimension_semantics=("parallel", "arbitrary")),
        name="lru_mix",
    )(p, cw, cb, wax, bax, lam)


def _ret_kernel(p_ref, cos_ref, sin_ref, dmat_ref, kdec_ref, qdec_ref, cross_ref, bd_ref, seg2_ref,
                gnw_ref, y_ref, state):
    c = pl.program_id(1)
    nb = p_ref.shape[0]
    W = RET_WIDTH

    @pl.when(c == 0)
    def _():
        state[...] = jnp.zeros(state.shape, F32)

    lane = lax.broadcasted_iota(jnp.int32, (1, RET_QK), 1)
    first_half = (lane % RET_QK_DIM) < (RET_QK_DIM // 2)
    half = RET_QK_DIM // 2
    qk_head = lane // RET_QK_DIM
    v_head = lax.broadcasted_iota(jnp.int32, (1, W), 1) // RET_V_DIM

    def rope(x):
        swapped = jnp.where(first_half, pltpu.roll(x, RET_QK - half, axis=1), pltpu.roll(x, half, axis=1))
        return x * cos_ref[...] + swapped * sin_ref[...]

    def prep(b):
        q = rope(p_ref[b, :, 0:RET_QK])
        k = rope(p_ref[b, :, RET_QK:2 * RET_QK]) * (RET_QK_DIM ** -0.5)
        v = p_ref[b, :, 2 * RET_QK:2 * RET_QK + W]
        return dict(q=q, k32=k, k=k.astype(BF16), v=v.astype(BF16), v32=v)

    def scores(d):
        d["s"] = [(_mm_nt(jnp.where(qk_head == h, d["q"], 0.0), d["k"]) * dmat_ref[h]).astype(BF16)
                  for h in range(RET_HEADS)]

    def outputs(b, d):
        y = _mm(d["q"] * qdec_ref[...], state[b])
        for h in range(RET_HEADS):
            y = y + jnp.dot(d["s"][h], jnp.where(v_head == h, d["v32"], 0.0).astype(BF16),
                            preferred_element_type=F32)
        d["y"] = y
        kv = _mm_tn(d["k32"] * kdec_ref[...], d["v"]) * bd_ref[...]
        state[b] = state[b] * cross_ref[...] + kv

    def center(d):
        d["yc"] = d["y"] - _seg_sum(d["y"], seg2_ref) * (1.0 / RET_V_DIM)

    def finish(b, d):
        var = _seg_sum(d["yc"] * d["yc"], seg2_ref) * (1.0 / RET_V_DIM)
        g = p_ref[b, :, 2 * RET_QK + W:2 * RET_QK + 2 * W]
        y_ref[b] = d["yc"] * lax.rsqrt(var + RET_GN_EPS) * gnw_ref[...] * _silu(g)

    ds = [prep(b) for b in range(nb)]
    for d in ds:
        scores(d)
    for b, d in enumerate(ds):
        outputs(b, d)
    for d in ds:
        center(d)
    for b, d in enumerate(ds):
        finish(b, d)


def _ret_mix(p, cos_t, sin_t, dmat, kdec, qdec, cross, bd, seg2, gnw, nb):
    bsz, tp, _ = p.shape
    const2 = lambda s: pl.BlockSpec(s, lambda b, c: (0, 0))
    return pl.pallas_call(
        _ret_kernel,
        grid=(bsz // nb, tp // CHUNK),
        in_specs=[pl.BlockSpec((nb, CHUNK, RET_IN), lambda b, c: (b, c, 0)),
                  pl.BlockSpec((CHUNK, RET_QK), lambda b, c: (c, 0)),
                  pl.BlockSpec((CHUNK, RET_QK), lambda b, c: (c, 0)),
                  pl.BlockSpec((RET_HEADS, CHUNK, CHUNK), lambda b, c: (0, 0, 0)),
                  const2((CHUNK, RET_QK)), const2((CHUNK, RET_QK)),
                  const2((RET_QK, RET_WIDTH)), const2((RET_QK, RET_WIDTH)),
                  const2((2 * RET_WIDTH, RET_WIDTH)), const2((1, RET_WIDTH))],
        out_specs=pl.BlockSpec((nb, CHUNK, RET_WIDTH), lambda b, c: (b, c, 0)),
        out_shape=jax.ShapeDtypeStruct((bsz, tp, RET_WIDTH), F32),
        scratch_shapes=[pltpu.VMEM((nb, RET_QK, RET_WIDTH), F32)],
        compiler_params=pltpu.CompilerParams(dimension_semantics=("parallel", "arbitrary")),
        name="ret_mix",
    )(p, cos_t, sin_t, dmat, kdec, qdec, cross, bd, seg2, gnw)


def _retention_tables(tp):
    half = RET_QK_DIM // 2
    lane = np.arange(RET_QK)
    freqs = ROPE_BASE ** (-np.arange(half, dtype=np.float64) / half)
    pos = np.arange(tp, dtype=np.float64) - PAD
    ang = pos[:, None] * freqs[lane % half][None, :]
    sign = np.where((lane % RET_QK_DIM) < half, -1.0, 1.0)
    cos_t = np.cos(ang)
    sin_t = np.sin(ang) * sign[None, :]
    log_g = np.log1p(-np.exp2(-5.0 - np.arange(RET_HEADS, dtype=np.float64)))
    idx = np.arange(CHUNK)
    rel = idx[:, None] - idx[None, :]
    dmat = np.where(rel >= 0, np.exp(np.maximum(rel, 0)[None] * log_g[:, None, None]), 0.0)
    lg_lane = log_g[lane // RET_QK_DIM]
    kdec = np.exp((CHUNK - 1 - idx)[:, None] * lg_lane[None, :])
    qdec = np.exp((idx + 1)[:, None] * lg_lane[None, :])
    cross = np.broadcast_to(np.exp(CHUNK * lg_lane)[:, None], (RET_QK, RET_WIDTH))
    v_head = np.arange(RET_WIDTH) // RET_V_DIM
    bd = (lane // RET_QK_DIM)[:, None] == v_head[None, :]
    seg = (v_head[:, None] == v_head[None, :]).astype(np.float32)
    f = lambda x: jnp.asarray(np.ascontiguousarray(x), F32)
    return (f(cos_t), f(sin_t), f(dmat), f(kdec), f(qdec), f(cross), f(bd),
            jnp.asarray(np.concatenate([seg, seg], axis=0), BF16))


def _out_ffn_kernel(h_ref, y0_ref, y1_ref, y2_ref, y3_ref, wo_ref, n1_ref, n2_ref, n3_ref,
                    wg_ref, wu_ref, wd_ref, o_ref, h1_s, hn_s, acc_s):
    j = pl.program_id(1)

    @pl.when(j == 0)
    def _():
        u = None
        for i, yr in enumerate((y0_ref, y1_ref, y2_ref, y3_ref)):
            part = _mm(yr[...], wo_ref[256 * i:256 * i + 256, :])
            u = part if u is None else u + part
        h1 = h_ref[...] + _rms(u, n1_ref[...])
        h1_s[...] = h1
        hn_s[...] = _rms(h1, n2_ref[...]).astype(BF16)
        acc_s[...] = jnp.zeros(acc_s.shape, F32)

    hn = hn_s[...]
    act = _silu(jnp.dot(hn, wg_ref[...], preferred_element_type=F32)) * \
        jnp.dot(hn, wu_ref[...], preferred_element_type=F32)
    acc_s[...] += jnp.dot(act.astype(BF16), wd_ref[...], preferred_element_type=F32)

    @pl.when(j == pl.num_programs(1) - 1)
    def _():
        o_ref[...] = h1_s[...] + _rms(acc_s[...], n3_ref[...])


def _out_ffn(h, ys, wo, n1, n2, n3, wg, wu, wd, tm, tf):
    m, d = h.shape
    row = lambda n: pl.BlockSpec((tm, n), lambda i, j: (i, 0))
    vec = pl.BlockSpec((1, d), lambda i, j: (0, 0))
    return pl.pallas_call(
        _out_ffn_kernel,
        grid=(m // tm, D_FF // tf),
        in_specs=[row(d), row(256), row(256), row(256), row(256),
                  pl.BlockSpec((d, d), lambda i, j: (0, 0)), vec, vec, vec,
                  pl.BlockSpec((d, tf), lambda i, j: (0, j)),
                  pl.BlockSpec((d, tf), lambda i, j: (0, j)),
                  pl.BlockSpec((tf, d), lambda i, j: (j, 0))],
        out_specs=row(d),
        out_shape=jax.ShapeDtypeStruct((m, d), F32),
        scratch_shapes=[pltpu.VMEM((tm, d), F32), pltpu.VMEM((tm, d), BF16), pltpu.VMEM((tm, d), F32)],
        compiler_params=pltpu.CompilerParams(
            dimension_semantics=("parallel", "arbitrary"), vmem_limit_bytes=VMEM_LIMIT_BYTES),
        name="out_ffn",
    )(h, *ys, wo, n1, n2, n3, wg, wu, wd)


def _block_diag(blocks):
    g, n, m = blocks.shape
    eye = jnp.eye(g, dtype=blocks.dtype)
    return (eye[:, None, :, None] * blocks[:, :, None, :]).reshape(g * n, g * m)


def _row_tile(tp):
    best = 8
    for t in range(8, 641, 8):
        if tp % t == 0:
            best = t
    return best


def kernel(x, meta_tokens, pre_mix_norm, post_mix_norm, pre_ffn_norm, post_ffn_norm, w_in, w_out, ssd_conv_w, ssd_conv_b, ssd_dt_bias, ssd_a_log, ssd_d, ssd_norm_w, rwkv_mu, rwkv_w0, rwkv_w2, rwkv_a0, rwkv_a2, rwkv_g2, rwkv_k_k, rwkv_k_a, rwkv_r_k, rwkv_ln_w, rwkv_ln_b, lru_conv_w, lru_conv_b, lru_wa, lru_ba, lru_wx, lru_bx, lru_lambda, ret_gn_w, ffn_w_gate, ffn_w_up, ffn_w_down):
    bsz, seq, d = x.shape
    depth = w_in.shape[0]
    t = N_META + seq
    tp = t + PAD
    assert d == D_MODEL and tp % CHUNK == 0
    meta = jnp.broadcast_to(meta_tokens.astype(x.dtype)[None], (bsz, N_META, d))
    h = jnp.concatenate([jnp.zeros((bsz, PAD, d), x.dtype), meta, x], axis=1)

    tm_in = _row_tile(tp)
    m_rows = bsz * tp
    tm_ffn = 1024 if m_rows % 1024 == 0 else CHUNK
    tf = 256
    tables = _retention_tables(tp)
    per_head = lambda v: jnp.repeat(v, SSD_HEAD_DIM, axis=-1)[None]
    r2 = lambda v: v[None]

    w_cat = _prep_w_in(w_in)
    nb = 4 if bsz % 4 == 0 else (2 if bsz % 2 == 0 else 1)
    nb_rwkv = nb

    for l in range(depth):
        p_ssd, p_rwkv, p_lru, p_ret = _in_proj(h, r2(pre_mix_norm[l]), w_cat[l], tm_in)

        y_ssd = _ssd_mix(p_ssd, ssd_conv_w[l], r2(ssd_conv_b[l]), per_head(ssd_dt_bias[l]),
                         per_head(ssd_a_log[l]), per_head(ssd_d[l]), r2(ssd_norm_w[l]), nb)

        zero = jnp.zeros((64, RWKV_WIDTH), F32)
        wa2 = jnp.concatenate([jnp.concatenate([rwkv_w2[l], zero], axis=1),
                               jnp.concatenate([zero, rwkv_a2[l]], axis=1)], axis=0).astype(BF16)
        y_rwkv = _rwkv_mix(p_rwkv, r2(rwkv_mu[l]), r2(rwkv_w0[l]), r2(rwkv_a0[l]), wa2,
                           rwkv_g2[l].astype(BF16), r2(rwkv_k_k[l]), r2(rwkv_k_a[l]),
                           rwkv_r_k[l].reshape(1, RWKV_WIDTH), r2(rwkv_ln_w[l]), r2(rwkv_ln_b[l]),
                           nb_rwkv)

        wax = jnp.concatenate([_block_diag(lru_wa[l]), _block_diag(lru_wx[l])], axis=1).astype(BF16)
        bax = jnp.concatenate([lru_ba[l], lru_bx[l]])[None]
        y_lru = _lru_mix(p_lru, lru_conv_w[l], r2(lru_conv_b[l]), wax, bax, r2(lru_lambda[l]), nb)

        y_ret = _ret_mix(p_ret, *tables, r2(ret_gn_w[l]), nb)

        flat = lambda a: a.reshape(m_rows, a.shape[-1])
        h = _out_ffn(flat(h), [flat(y_ssd), flat(y_rwkv), flat(y_lru), flat(y_ret)],
                     w_out[l].astype(BF16), r2(post_mix_norm[l]), r2(pre_ffn_norm[l]),
                     r2(post_ffn_norm[l]), ffn_w_gate[l].astype(BF16), ffn_w_up[l].astype(BF16),
                     ffn_w_down[l].astype(BF16), tm_ffn, tf).reshape(bsz, tp, d)
    return h[:, PAD + N_META:]
```

```python
import functools
import math

import numpy as np
import jax
import jax.numpy as jnp
from jax import lax
from jax.experimental import pallas as pl
from jax.experimental.pallas import tpu as pltpu

F32 = jnp.float32
BF16 = jnp.bfloat16

D_MODEL = 1024
N_META = 16
CHUNK = 128
PAD = CHUNK - N_META
NORM_EPS = 1e-6

SSD_HEADS = 4
SSD_HEAD_DIM = 64
SSD_WIDTH = 256
SSD_STATE = 128
SSD_CONV = 4
SSD_CONV_CH = 768
SSD_IN = 1028

RWKV_HEADS = 4
RWKV_HEAD_DIM = 64
RWKV_WIDTH = 256
RWKV_IN = 1024
RWKV_GN_EPS = 64e-5
RWKV_CHUNK = 64

LRU_WIDTH = 256
LRU_CONV = 4
LRU_C = 8.0
LRU_IN = 512

RET_HEADS = 4
RET_QK_DIM = 32
RET_V_DIM = 64
RET_WIDTH = 256
RET_QK = RET_HEADS * RET_QK_DIM
RET_IN = 768
RET_GN_EPS = 1e-5
ROPE_BASE = 10000.0

D_FF = 2816

VMEM_LIMIT_BYTES = 52 * 1024 * 1024

P_SSD_W = SSD_WIDTH + SSD_CONV_CH + SSD_WIDTH
IN_COLS = P_SSD_W + RWKV_IN + LRU_IN + RET_IN


def _mm(a, b):
    return jnp.dot(a.astype(BF16), b.astype(BF16), preferred_element_type=F32)


def _mm_nt(a, b):
    return lax.dot_general(a.astype(BF16), b.astype(BF16), (((1,), (1,)), ((), ())),
                           preferred_element_type=F32)


def _mm_tn(a, b):
    return lax.dot_general(a.astype(BF16), b.astype(BF16), (((0,), (0,)), ((), ())),
                           preferred_element_type=F32)


def _mm_exact(a, b):
    return jnp.dot(a, b, preferred_element_type=F32, precision=lax.Precision.HIGHEST)


def _sigmoid(x):
    return 1.0 / (1.0 + jnp.exp(-x))


def _silu(x):
    return x * _sigmoid(x)


def _softplus(x):
    return jnp.maximum(x, 0.0) + jnp.log1p(jnp.exp(-jnp.abs(x)))


def _rms(x, w):
    return x * lax.rsqrt(jnp.mean(x * x, axis=-1, keepdims=True) + NORM_EPS) * w


def _causal_conv(buf, w, b, rows):
    acc = b + w[3:4, :] * buf[8:8 + rows, :]
    acc = acc + w[2:3, :] * buf[7:7 + rows, :]
    acc = acc + w[1:2, :] * buf[6:6 + rows, :]
    acc = acc + w[0:1, :] * buf[5:5 + rows, :]
    return acc


def _prep_w_in_kernel(w_ref, o_ref):
    main = SSD_WIDTH + SSD_CONV_CH
    o_ref[:, 0:main] = w_ref[:, 0:main].astype(BF16)
    lane_lo = lax.broadcasted_iota(jnp.int32, (1, 128), 1) < SSD_HEAD_DIM
    for g in range(SSD_HEADS // 2):
        c0 = w_ref[:, main + 2 * g:main + 2 * g + 1]
        c1 = w_ref[:, main + 2 * g + 1:main + 2 * g + 2]
        o_ref[:, main + 128 * g:main + 128 * g + 128] = jnp.where(lane_lo, c0, c1).astype(BF16)
    o_ref[:, P_SSD_W:] = w_ref[:, SSD_IN:].astype(BF16)


def _prep_w_in(w_in):
    depth, d, n = w_in.shape
    tr = 256
    return pl.pallas_call(
        _prep_w_in_kernel,
        grid=(depth, d // tr),
        in_specs=[pl.BlockSpec((None, tr, n), lambda l, i: (l, i, 0))],
        out_specs=pl.BlockSpec((None, tr, IN_COLS), lambda l, i: (l, i, 0)),
        out_shape=jax.ShapeDtypeStruct((depth, d, IN_COLS), BF16),
        compiler_params=pltpu.CompilerParams(dimension_semantics=("parallel", "parallel")),
        name="prep_w_in",
    )(w_in)


def _in_proj_kernel(h_ref, nw_ref, w_ref, ssd_ref, rwkv_ref, lru_ref, ret_ref):
    tm = h_ref.shape[0]
    row = pl.program_id(1) * tm + lax.broadcasted_iota(jnp.int32, (tm, 1), 0)
    hn = _rms(h_ref[...], nw_ref[...])
    hn = jnp.where(row >= PAD, hn, 0.0).astype(BF16)
    o = 0
    for ref in (ssd_ref, rwkv_ref, lru_ref, ret_ref):
        n = ref.shape[1]
        ref[...] = jnp.dot(hn, w_ref[:, o:o + n], preferred_element_type=F32)
        o += n


def _in_proj(h, nw, w, tm):
    bsz, tp, d = h.shape
    widths = (P_SSD_W, RWKV_IN, LRU_IN, RET_IN)
    return pl.pallas_call(
        _in_proj_kernel,
        grid=(bsz, tp // tm),
        in_specs=[pl.BlockSpec((None, tm, d), lambda b, i: (b, i, 0)),
                  pl.BlockSpec((1, d), lambda b, i: (0, 0)),
                  pl.BlockSpec((d, IN_COLS), lambda b, i: (0, 0))],
        out_specs=[pl.BlockSpec((None, tm, n), lambda b, i: (b, i, 0)) for n in widths],
        out_shape=[jax.ShapeDtypeStruct((bsz, tp, n), F32) for n in widths],
        compiler_params=pltpu.CompilerParams(
            dimension_semantics=("parallel", "arbitrary"), vmem_limit_bytes=VMEM_LIMIT_BYTES),
        name="in_proj",
    )(h, nw, w)


def _split_bf16(x, parts):
    out = []
    for _ in range(parts):
        hi = x.astype(BF16)
        out.append(hi)
        x = x - hi.astype(F32)
    return out


def _cumsum_rows(x, tril3_ref):
    return jnp.dot(tril3_ref[...], jnp.concatenate(_split_bf16(x, 3), axis=0), preferred_element_type=F32)


def _tril3(n):
    tril = (np.arange(n)[:, None] >= np.arange(n)[None, :]).astype(np.float32)
    return jnp.asarray(np.concatenate([tril] * 3, axis=1), BF16)


def _ssd_kernel(p_ref, cw_ref, cb_ref, dtb_ref, alog_ref, dsk_ref, nw_ref, tril3_ref, y_ref, xbuf, state):
    c = pl.program_id(1)
    nb = p_ref.shape[0]
    L = CHUNK

    @pl.when(c == 0)
    def _():
        xbuf[:, 0:8, :] = jnp.zeros((nb, 8, SSD_CONV_CH), F32)
        state[...] = jnp.zeros(state.shape, F32)

    row = c * L + lax.broadcasted_iota(jnp.int32, (L, 1), 0)
    causal = lax.broadcasted_iota(jnp.int32, (L, L), 0) >= lax.broadcasted_iota(jnp.int32, (L, L), 1)
    lane_lo = lax.broadcasted_iota(jnp.int32, (1, 128), 1) < SSD_HEAD_DIM
    groups = [slice(128 * g, 128 * g + 128) for g in range(2)]

    def prep(b):
        buf = xbuf.at[b]
        buf[8:8 + L, :] = p_ref[b, :, SSD_WIDTH:SSD_WIDTH + SSD_CONV_CH]
        xbc = _silu(_causal_conv(buf, cw_ref[...], cb_ref[...], L))
        buf[0:8, :] = buf[L:L + 8, :]
        dt = _softplus(p_ref[b, :, SSD_WIDTH + SSD_CONV_CH:P_SSD_W] + dtb_ref[...])
        dt = jnp.where(row >= PAD, dt, 0.0)
        a = dt * (-jnp.exp(alog_ref[...]))
        xs = xbc[:, 0:256]
        return dict(xs=xs, bs=xbc[:, 256:512], cs=xbc[:, 512:768], xdt=xs * dt,
                    acs=_cumsum_rows(a, tril3_ref), y=[None, None])

    def gram(d, g):
        sl = groups[g]
        return _mm_nt(d["cs"][:, sl], d["bs"][:, sl])

    def chunk(b, d, g, gmat):
        sl = groups[g]
        acs_g = d["acs"][:, sl]
        acs_t = acs_g.T
        a_last = acs_g[L - 1:L, :]
        b_g, c_g, x_g = d["bs"][:, sl], d["cs"][:, sl], d["xdt"][:, sl]
        ms = []
        for hh in range(2):
            o = SSD_HEAD_DIM * hh
            diff = acs_g[:, o:o + 1] - acs_t[o:o + 1, :]
            ms.append(gmat * jnp.exp(jnp.where(causal, diff, -1e30)))
        mcat = jnp.concatenate(ms, axis=1)
        xbd = jnp.concatenate([jnp.where(lane_lo, x_g, 0.0), jnp.where(lane_lo, 0.0, x_g)], axis=0)
        s_in = state[b, g]
        d["y"][g] = _mm(mcat, xbd) + _mm(c_g, s_in) * jnp.exp(acs_g)
        state[b, g] = s_in * jnp.exp(a_last) + _mm_tn(b_g, x_g * jnp.exp(a_last - acs_g))

    def finish(b, d, g):
        sl = groups[g]
        y_g = (d["y"][g] + dsk_ref[:, sl] * d["xs"][:, sl]) * _silu(p_ref[b, :, sl])
        y_ref[b, :, sl] = _rms(y_g, nw_ref[:, sl])

    ds = [prep(b) for b in range(nb)]
    gm = [[gram(d, g) for g in range(2)] for d in ds]
    for b, d in enumerate(ds):
        for g in range(2):
            chunk(b, d, g, gm[b][g])
    for b, d in enumerate(ds):
        for g in range(2):
            finish(b, d, g)


def _ssd_mix(p, cw, cb, dtb, alog, dsk, nw, nb):
    bsz, tp, _ = p.shape
    vec = lambda n: pl.BlockSpec((1, n), lambda b, c: (0, 0))
    return pl.pallas_call(
        _ssd_kernel,
        grid=(bsz // nb, tp // CHUNK),
        in_specs=[pl.BlockSpec((nb, CHUNK, P_SSD_W), lambda b, c: (b, c, 0)),
                  pl.BlockSpec((SSD_CONV, SSD_CONV_CH), lambda b, c: (0, 0)),
                  vec(SSD_CONV_CH), vec(SSD_WIDTH), vec(SSD_WIDTH), vec(SSD_WIDTH), vec(SSD_WIDTH),
                  pl.BlockSpec((CHUNK, 3 * CHUNK), lambda b, c: (0, 0))],
        out_specs=pl.BlockSpec((nb, CHUNK, SSD_WIDTH), lambda b, c: (b, c, 0)),
        out_shape=jax.ShapeDtypeStruct((bsz, tp, SSD_WIDTH), F32),
        scratch_shapes=[pltpu.VMEM((nb, CHUNK + 8, SSD_CONV_CH), F32),
                        pltpu.VMEM((nb, 2, SSD_STATE, 128), F32)],
        compiler_params=pltpu.CompilerParams(dimension_semantics=("parallel", "arbitrary")),
        name="ssd_mix",
    )(p, cw, cb, dtb, alog, dsk, nw, _tril3(CHUNK))


def _seg_sum(x, seg2_ref):
    return jnp.dot(jnp.concatenate(_split_bf16(x, 2), axis=1), seg2_ref[...], preferred_element_type=F32)


def _rwkv_kernel(p_ref, mu_ref, w0_ref, a0_ref, wa2_ref, g2_ref, kk_ref, ka_ref, rk_ref,
                 lnw_ref, lnb_ref, seg2_ref, tril3_ref, mask_ref, maskw_ref, y_ref, pbuf, state):
    c = pl.program_id(1)
    nb = p_ref.shape[0]
    C = RWKV_CHUNK
    W = RWKV_WIDTH
    R = RWKV_HEADS * C

    @pl.when(c == 0)
    def _():
        pbuf[:, 0:8, :] = jnp.zeros((nb, 8, RWKV_IN), F32)
        state[...] = jnp.zeros(state.shape, F32)

    lane = lax.broadcasted_iota(jnp.int32, (1, 128), 1)
    lane_head = lax.broadcasted_iota(jnp.int32, (1, W), 1) // RWKV_HEAD_DIM

    def stack(x):
        return jnp.concatenate([jnp.where(lane_head == h, x, 0.0) for h in range(RWKV_HEADS)], axis=0)

    def tile(x):
        return jnp.concatenate([x] * RWKV_HEADS, axis=0)

    def prep(b):
        pbuf[b, 8:8 + C, :] = p_ref[b]
        p = pbuf[b, 8:8 + C, :]
        p = p + (pbuf[b, 7:7 + C, :] - p) * mu_ref[...]
        pbuf[b, 0:8, :] = pbuf[b, C:C + 8, :]
        r, k, v = p[:, 0:W], p[:, W:2 * W], p[:, 2 * W:3 * W]
        lat = p[:, 3 * W:3 * W + 128]
        lat = jnp.where(lane < 64, jnp.tanh(lat), lat)
        wa = _mm(lat, wa2_ref[...])
        w = -_softplus(-(w0_ref[...] + wa[:, 0:W])) - 0.5
        logw = -jnp.exp(w)
        a = _sigmoid(a0_ref[...] + wa[:, W:2 * W])
        g = _mm(_sigmoid(p[:, 3 * W + 128:3 * W + 256]), g2_ref[...])
        kk = k * kk_ref[...]
        kk = kk / jnp.maximum(jnp.sqrt(_seg_sum(kk * kk, seg2_ref)), 1e-12)
        k2 = k * (1.0 + (a - 1.0) * ka_ref[...])
        bonus = _seg_sum(r * k2 * rk_ref[...], seg2_ref) * v
        cum = _cumsum_rows(logw, tril3_ref)
        e_neg = jnp.exp(-cum)
        pc = jnp.exp(cum[C - 1:C, :])
        bt = kk * a * e_neg
        kt = k2 * e_neg
        return dict(
            lhs=jnp.concatenate([-kk * jnp.exp(cum - logw), r * jnp.exp(cum)], axis=0).astype(BF16),
            rhs=jnp.concatenate([stack(bt), stack(kt)], axis=0).astype(BF16),
            upd=jnp.concatenate([bt * pc, kt * pc], axis=0).astype(BF16),
            v=v.astype(BF16), v_bd=(tile(v) * mask_ref[2]).astype(BF16), pc=pc, bonus=bonus, g=g)

    def scores(d):
        sc = lax.dot_general(d["lhs"], d["rhs"], (((1,), (1,)), ((), ())), preferred_element_type=F32)
        strict_w, incl_w = maskw_ref[0], maskw_ref[1]
        d["npow"] = tile(sc[0:C, 0:R]) * mask_ref[0]
        d["a_k"] = jnp.concatenate([sc[0:C, R:2 * R] * strict_w, sc[C:2 * C, R:2 * R] * incl_w],
                                   axis=0).astype(BF16)
        d["a_rb"] = (sc[C:2 * C, 0:R] * incl_w).astype(BF16)
        d["tinv"] = mask_ref[1] + d["npow"]

    def double(d):
        n16 = d["npow"].astype(BF16)
        d["npow"] = jnp.dot(n16, n16, preferred_element_type=F32)
        d["tinv"] = d["tinv"] + _mm(d["tinv"], d["npow"])

    def read_state(b, d):
        d["sxav"] = _mm_nt(d["lhs"], state[b]) + jnp.dot(d["a_k"], d["v_bd"], preferred_element_type=F32)

    def correction(d):
        x_bd = tile(d["sxav"][0:C]) * mask_ref[2]
        d["u"] = _mm(d["tinv"], x_bd)

    def write_state(b, d):
        u = d["u"]
        d["y"] = d["sxav"][C:2 * C] + jnp.dot(d["a_rb"], u.astype(BF16), preferred_element_type=F32)
        u_w = (u[0:C] + u[C:2 * C] + u[2 * C:3 * C] + u[3 * C:4 * C]).astype(BF16)
        new = _mm_tn(jnp.concatenate([u_w, d["v"]], axis=0), d["upd"])
        state[b] = state[b] * d["pc"] + new * mask_ref[3]

    def finish(b, d):
        inv_n = 1.0 / RWKV_HEAD_DIM
        yc = d["y"] - _seg_sum(d["y"], seg2_ref) * inv_n
        var = _seg_sum(yc * yc, seg2_ref) * inv_n
        yn = yc * lax.rsqrt(var + RWKV_GN_EPS) * lnw_ref[...] + lnb_ref[...]
        y_ref[b] = (yn + d["bonus"]) * d["g"]

    ds = [prep(b) for b in range(nb)]
    for d in ds:
        scores(d)
    n = 1
    while 2 * n < C:
        for d in ds:
            double(d)
        n *= 2
    for b, d in enumerate(ds):
        read_state(b, d)
    for d in ds:
        correction(d)
    for b, d in enumerate(ds):
        write_state(b, d)
    for b, d in enumerate(ds):
        finish(b, d)


def _rwkv_tables():
    C, W, R = RWKV_CHUNK, RWKV_WIDTH, RWKV_HEADS * RWKV_CHUNK
    head = np.arange(W) // RWKV_HEAD_DIM
    seg = (head[:, None] == head[None, :]).astype(np.float32)
    assert R == W
    rr, cc = np.arange(R)[:, None], np.arange(R)[None, :]
    strict = ((rr // C) == (cc // C)) & ((cc % C) < (rr % C))
    bdv = (rr // C) == (np.arange(W)[None, :] // RWKV_HEAD_DIM)
    tt = np.arange(C)[:, None]
    wide = np.stack([(cc % C) < tt, (cc % C) <= tt])
    return (jnp.asarray(np.concatenate([seg, seg], axis=0), BF16), _tril3(C),
            jnp.asarray(np.stack([strict, rr == cc, bdv, seg > 0]).astype(np.float32), F32),
            jnp.asarray(wide.astype(np.float32), F32))


def _rwkv_mix(p, mu, w0, a0, wa2, g2, kk, ka, rk, lnw, lnb, nb):
    bsz, tp, _ = p.shape
    C, W, R = RWKV_CHUNK, RWKV_WIDTH, RWKV_HEADS * RWKV_CHUNK
    vec = lambda n: pl.BlockSpec((1, n), lambda b, c: (0, 0))
    full = lambda s: pl.BlockSpec(s, lambda b, c: (0,) * len(s))
    return pl.pallas_call(
        _rwkv_kernel,
        grid=(bsz // nb, tp // C),
        in_specs=[pl.BlockSpec((nb, C, RWKV_IN), lambda b, c: (b, c, 0)),
                  vec(RWKV_IN), vec(W), vec(W), full((128, 2 * W)), full((128, W)),
                  vec(W), vec(W), vec(W), vec(W), vec(W),
                  full((2 * W, W)), full((C, 3 * C)), full((4, R, R)), full((2, C, R))],
        out_specs=pl.BlockSpec((nb, C, W), lambda b, c: (b, c, 0)),
        out_shape=jax.ShapeDtypeStruct((bsz, tp, W), F32),
        scratch_shapes=[pltpu.VMEM((nb, C + 8, RWKV_IN), F32),
                        pltpu.VMEM((nb, W, W), F32)],
        compiler_params=pltpu.CompilerParams(dimension_semantics=("parallel", "arbitrary")),
        name="rwkv_mix",
    )(p, mu, w0, a0, wa2, g2, kk, ka, rk, lnw, lnb, *_rwkv_tables())


def _lru_kernel(p_ref, cw_ref, cb_ref, wax_ref, bax_ref, lam_ref, y_ref, xbuf, hprev):
    c = pl.program_id(1)
    nb = p_ref.shape[0]
    L = CHUNK
    W = LRU_WIDTH

    @pl.when(c == 0)
    def _():
        xbuf[:, 0:8, :] = jnp.zeros((nb, 8, W), F32)
        hprev[...] = jnp.zeros(hprev.shape, F32)

    rowi = lax.broadcasted_iota(jnp.int32, (L, 1), 0)
    log_a_unit = -LRU_C * _softplus(-lam_ref[...])
    for b in range(nb):
        buf = xbuf.at[b]
        buf[8:8 + L, :] = p_ref[b, :, 0:W]
        xc = _causal_conv(buf, cw_ref[...], cb_ref[...], L)
        buf[0:8, :] = buf[L:L + 8, :]
        gates = _sigmoid(_mm(xc, wax_ref[...]) + bax_ref[...])
        r, i = gates[:, 0:W], gates[:, W:2 * W]
        log_a = r * log_a_unit
        a = jnp.exp(log_a)
        t = jnp.tanh(log_a)
        one_minus_a2 = -2.0 * t / (1.0 - t)
        u = jnp.sqrt(one_minus_a2) * (i * xc)
        u = jnp.where(c * L + rowi >= PAD, u, 0.0)

        s = 1
        while s < L:
            keep = rowi >= s
            a_sh = jnp.where(keep, pltpu.roll(a, s, axis=0), 1.0)
            u_sh = jnp.where(keep, pltpu.roll(u, s, axis=0), 0.0)
            u = u + a * u_sh
            a = a * a_sh
            s *= 2
        h = u + a * hprev[b, 0:1, :]
        hprev[b, 0:1, :] = h[L - 1:L, :]
        y_ref[b] = h * jax.nn.gelu(p_ref[b, :, W:2 * W], approximate=True)


def _lru_mix(p, cw, cb, wax, bax, lam, nb):
    bsz, tp, _ = p.shape
    vec = lambda n: pl.BlockSpec((1, n), lambda b, c: (0, 0))
    return pl.pallas_call(
        _lru_kernel,
        grid=(bsz // nb, tp // CHUNK),
        in_specs=[pl.BlockSpec((nb, CHUNK, LRU_IN), lambda b, c: (b, c, 0)),
                  pl.BlockSpec((LRU_CONV, LRU_WIDTH), lambda b, c: (0, 0)),
                  vec(LRU_WIDTH),
                  pl.BlockSpec((LRU_WIDTH, 2 * LRU_WIDTH), lambda b, c: (0, 0)),
                  vec(2 * LRU_WIDTH), vec(LRU_WIDTH)],
        out_specs=pl.BlockSpec((nb, CHUNK, LRU_WIDTH), lambda b, c: (b, c, 0)),
        out_shape=jax.ShapeDtypeStruct((bsz, tp, LRU_WIDTH), F32),
        scratch_shapes=[pltpu.VMEM((nb, CHUNK + 8, LRU_WIDTH), F32),
                        pltpu.VMEM((nb, 8, LRU_WIDTH), F32)],
        compiler_params=pltpu.CompilerParams(dimension_semantics=("parallel", "arbitrary")),
        name="lru_mix",
    )(p, cw, cb, wax, bax, lam)


def _ret_kernel(p_ref, cos_ref, sin_ref, dmat_ref, kdec_ref, qdec_ref, cross_ref, bd_ref, seg2_ref,
                gnw_ref, y_ref, state):
    c = pl.program_id(1)
    nb = p_ref.shape[0]
    W = RET_WIDTH

    @pl.when(c == 0)
    def _():
        state[...] = jnp.zeros(state.shape, F32)

    lane = lax.broadcasted_iota(jnp.int32, (1, RET_QK), 1)
    first_half = (lane % RET_QK_DIM) < (RET_QK_DIM // 2)
    half = RET_QK_DIM // 2
    qk_head = lane // RET_QK_DIM
    v_head = lax.broadcasted_iota(jnp.int32, (1, W), 1) // RET_V_DIM

    def rope(x):
        swapped = jnp.where(first_half, pltpu.roll(x, RET_QK - half, axis=1), pltpu.roll(x, half, axis=1))
        return x * cos_ref[...] + swapped * sin_ref[...]

    def prep(b):
        q = rope(p_ref[b, :, 0:RET_QK])
        k = rope(p_ref[b, :, RET_QK:2 * RET_QK]) * (RET_QK_DIM ** -0.5)
        v = p_ref[b, :, 2 * RET_QK:2 * RET_QK + W]
        return dict(q=q, k32=k, k=k.astype(BF16), v=v.astype(BF16), v32=v)

    def scores(d):
        d["s"] = [(_mm_nt(jnp.where(qk_head == h, d["q"], 0.0), d["k"]) * dmat_ref[h]).astype(BF16)
                  for h in range(RET_HEADS)]

    def outputs(b, d):
        y = _mm(d["q"] * qdec_ref[...], state[b])
        for h in range(RET_HEADS):
            y = y + jnp.dot(d["s"][h], jnp.where(v_head == h, d["v32"], 0.0).astype(BF16),
                            preferred_element_type=F32)
        d["y"] = y
        kv = _mm_tn(d["k32"] * kdec_ref[...], d["v"]) * bd_ref[...]
        state[b] = state[b] * cross_ref[...] + kv

    def center(d):
        d["yc"] = d["y"] - _seg_sum(d["y"], seg2_ref) * (1.0 / RET_V_DIM)

    def finish(b, d):
        var = _seg_sum(d["yc"] * d["yc"], seg2_ref) * (1.0 / RET_V_DIM)
        g = p_ref[b, :, 2 * RET_QK + W:2 * RET_QK + 2 * W]
        y_ref[b] = d["yc"] * lax.rsqrt(var + RET_GN_EPS) * gnw_ref[...] * _silu(g)

    ds = [prep(b) for b in range(nb)]
    for d in ds:
        scores(d)
    for b, d in enumerate(ds):
        outputs(b, d)
    for d in ds:
        center(d)
    for b, d in enumerate(ds):
        finish(b, d)


def _ret_mix(p, cos_t, sin_t, dmat, kdec, qdec, cross, bd, seg2, gnw, nb):
    bsz, tp, _ = p.shape
    const2 = lambda s: pl.BlockSpec(s, lambda b, c: (0, 0))
    return pl.pallas_call(
        _ret_kernel,
        grid=(bsz // nb, tp // CHUNK),
        in_specs=[pl.BlockSpec((nb, CHUNK, RET_IN), lambda b, c: (b, c, 0)),
                  pl.BlockSpec((CHUNK, RET_QK), lambda b, c: (c, 0)),
                  pl.BlockSpec((CHUNK, RET_QK), lambda b, c: (c, 0)),
                  pl.BlockSpec((RET_HEADS, CHUNK, CHUNK), lambda b, c: (0, 0, 0)),
                  const2((CHUNK, RET_QK)), const2((CHUNK, RET_QK)),
                  const2((RET_QK, RET_WIDTH)), const2((RET_QK, RET_WIDTH)),
                  const2((2 * RET_WIDTH, RET_WIDTH)), const2((1, RET_WIDTH))],
        out_specs=pl.BlockSpec((nb, CHUNK, RET_WIDTH), lambda b, c: (b, c, 0)),
        out_shape=jax.ShapeDtypeStruct((bsz, tp, RET_WIDTH), F32),
        scratch_shapes=[pltpu.VMEM((nb, RET_QK, RET_WIDTH), F32)],
        compiler_params=pltpu.CompilerParams(dimension_semantics=("parallel", "arbitrary")),
        name="ret_mix",
    )(p, cos_t, sin_t, dmat, kdec, qdec, cross, bd, seg2, gnw)


def _retention_tables(tp):
    half = RET_QK_DIM // 2
    lane = np.arange(RET_QK)
    freqs = ROPE_BASE ** (-np.arange(half, dtype=np.float64) / half)
    pos = np.arange(tp, dtype=np.float64) - PAD
    ang = pos[:, None] * freqs[lane % half][None, :]
    sign = np.where((lane % RET_QK_DIM) < half, -1.0, 1.0)
    cos_t = np.cos(ang)
    sin_t = np.sin(ang) * sign[None, :]
    log_g = np.log1p(-np.exp2(-5.0 - np.arange(RET_HEADS, dtype=np.float64)))
    idx = np.arange(CHUNK)
    rel = idx[:, None] - idx[None, :]
    dmat = np.where(rel >= 0, np.exp(np.maximum(rel, 0)[None] * log_g[:, None, None]), 0.0)
    lg_lane = log_g[lane // RET_QK_DIM]
    kdec = np.exp((CHUNK - 1 - idx)[:, None] * lg_lane[None, :])
    qdec = np.exp((idx + 1)[:, None] * lg_lane[None, :])
    cross = np.broadcast_to(np.exp(CHUNK * lg_lane)[:, None], (RET_QK, RET_WIDTH))
    v_head = np.arange(RET_WIDTH) // RET_V_DIM
    bd = (lane // RET_QK_DIM)[:, None] == v_head[None, :]
    seg = (v_head[:, None] == v_head[None, :]).astype(np.float32)
    f = lambda x: jnp.asarray(np.ascontiguousarray(x), F32)
    return (f(cos_t), f(sin_t), f(dmat), f(kdec), f(qdec), f(cross), f(bd),
            jnp.asarray(np.concatenate([seg, seg], axis=0), BF16))


def _out_ffn_kernel(h_ref, y0_ref, y1_ref, y2_ref, y3_ref, wo_ref, n1_ref, n2_ref, n3_ref,
                    wg_ref, wu_ref, wd_ref, o_ref, *, tf):
    u = None
    for i, yr in enumerate((y0_ref, y1_ref, y2_ref, y3_ref)):
        part = _mm(yr[...], wo_ref[256 * i:256 * i + 256, :])
        u = part if u is None else u + part
    h1 = h_ref[...] + _rms(u, n1_ref[...])
    hn = _rms(h1, n2_ref[...]).astype(BF16)
    acc = None
    for j in range(D_FF // tf):
        sl = slice(j * tf, (j + 1) * tf)
        act = _silu(jnp.dot(hn, wg_ref[:, sl], preferred_element_type=F32)) * \
            jnp.dot(hn, wu_ref[:, sl], preferred_element_type=F32)
        part = jnp.dot(act.astype(BF16), wd_ref[sl, :], preferred_element_type=F32)
        acc = part if acc is None else acc + part
    o_ref[...] = h1 + _rms(acc, n3_ref[...])


def _out_ffn(h, ys, wo, n1, n2, n3, wg, wu, wd, tm, tf):
    m, d = h.shape
    row = lambda n: pl.BlockSpec((tm, n), lambda i: (i, 0))
    full = lambda s: pl.BlockSpec(s, lambda i: (0, 0))
    return pl.pallas_call(
        functools.partial(_out_ffn_kernel, tf=tf),
        grid=(m // tm,),
        in_specs=[row(d), row(256), row(256), row(256), row(256),
                  full((d, d)), full((1, d)), full((1, d)), full((1, d)),
                  full((d, D_FF)), full((d, D_FF)), full((D_FF, d))],
        out_specs=row(d),
        out_shape=jax.ShapeDtypeStruct((m, d), F32),
        compiler_params=pltpu.CompilerParams(
            dimension_semantics=("parallel",), vmem_limit_bytes=VMEM_LIMIT_BYTES),
        name="out_ffn",
    )(h, *ys, wo, n1, n2, n3, wg, wu, wd)


def _block_diag(blocks):
    g, n, m = blocks.shape
    eye = jnp.eye(g, dtype=blocks.dtype)
    return (eye[:, None, :, None] * blocks[:, :, None, :]).reshape(g * n, g * m)


def _row_tile(tp):
    best = 8
    for t in range(8, 641, 8):
        if tp % t == 0:
            best = t
    return best


def kernel(x, meta_tokens, pre_mix_norm, post_mix_norm, pre_ffn_norm, post_ffn_norm, w_in, w_out, ssd_conv_w, ssd_conv_b, ssd_dt_bias, ssd_a_log, ssd_d, ssd_norm_w, rwkv_mu, rwkv_w0, rwkv_w2, rwkv_a0, rwkv_a2, rwkv_g2, rwkv_k_k, rwkv_k_a, rwkv_r_k, rwkv_ln_w, rwkv_ln_b, lru_conv_w, lru_conv_b, lru_wa, lru_ba, lru_wx, lru_bx, lru_lambda, ret_gn_w, ffn_w_gate, ffn_w_up, ffn_w_down):
    bsz, seq, d = x.shape
    depth = w_in.shape[0]
    t = N_META + seq
    tp = t + PAD
    assert d == D_MODEL and tp % CHUNK == 0
    meta = jnp.broadcast_to(meta_tokens.astype(x.dtype)[None], (bsz, N_META, d))
    h = jnp.concatenate([jnp.zeros((bsz, PAD, d), x.dtype), meta, x], axis=1)

    tm_in = _row_tile(tp)
    m_rows = bsz * tp
    tm_ffn = 512 if m_rows % 512 == 0 else CHUNK
    tf = 256
    tables = _retention_tables(tp)
    per_head = lambda v: jnp.repeat(v, SSD_HEAD_DIM, axis=-1)[None]
    r2 = lambda v: v[None]

    w_cat = _prep_w_in(w_in)
    nb = 4 if bsz % 4 == 0 else (2 if bsz % 2 == 0 else 1)
    nb_rwkv = 8 if bsz % 8 == 0 else nb

    for l in range(depth):
        p_ssd, p_rwkv, p_lru, p_ret = _in_proj(h, r2(pre_mix_norm[l]), w_cat[l], tm_in)

        y_ssd = _ssd_mix(p_ssd, ssd_conv_w[l], r2(ssd_conv_b[l]), per_head(ssd_dt_bias[l]),
                         per_head(ssd_a_log[l]), per_head(ssd_d[l]), r2(ssd_norm_w[l]), nb)

        zero = jnp.zeros((64, RWKV_WIDTH), F32)
        wa2 = jnp.concatenate([jnp.concatenate([rwkv_w2[l], zero], axis=1),
                               jnp.concatenate([zero, rwkv_a2[l]], axis=1)], axis=0).astype(BF16)
        y_rwkv = _rwkv_mix(p_rwkv, r2(rwkv_mu[l]), r2(rwkv_w0[l]), r2(rwkv_a0[l]), wa2,
                           rwkv_g2[l].astype(BF16), r2(rwkv_k_k[l]), r2(rwkv_k_a[l]),
                           rwkv_r_k[l].reshape(1, RWKV_WIDTH), r2(rwkv_ln_w[l]), r2(rwkv_ln_b[l]),
                           nb_rwkv)

        wax = jnp.concatenate([_block_diag(lru_wa[l]), _block_diag(lru_wx[l])], axis=1).astype(BF16)
        bax = jnp.concatenate([lru_ba[l], lru_bx[l]])[None]
        y_lru = _lru_mix(p_lru, lru_conv_w[l], r2(lru_conv_b[l]), wax, bax, r2(lru_lambda[l]), nb)

        y_ret = _ret_mix(p_ret, *tables, r2(ret_gn_w[l]), nb)

        flat = lambda a: a.reshape(m_rows, a.shape[-1])
        h = _out_ffn(flat(h), [flat(y_ssd), flat(y_rwkv), flat(y_lru), flat(y_ret)],
                     w_out[l].astype(BF16), r2(post_mix_norm[l]), r2(pre_ffn_norm[l]),
                     r2(post_ffn_norm[l]), ffn_w_gate[l].astype(BF16), ffn_w_up[l].astype(BF16),
                     ffn_w_down[l].astype(BF16), tm_ffn, tf).reshape(bsz, tp, d)
    return h[:, PAD + N_META:]
```

```python
import functools
import math

import numpy as np
import jax
import jax.numpy as jnp
from jax import lax
from jax.experimental import pallas as pl
from jax.experimental.pallas import tpu as pltpu

F32 = jnp.float32
BF16 = jnp.bfloat16

D_MODEL = 1024
N_META = 16
CHUNK = 128
PAD = CHUNK - N_META
NORM_EPS = 1e-6

SSD_HEADS = 4
SSD_HEAD_DIM = 64
SSD_WIDTH = 256
SSD_STATE = 128
SSD_CONV = 4
SSD_CONV_CH = 768
SSD_IN = 1028

RWKV_HEADS = 4
RWKV_HEAD_DIM = 64
RWKV_WIDTH = 256
RWKV_IN = 1024
RWKV_GN_EPS = 64e-5
RWKV_CHUNK = 64

LRU_WIDTH = 256
LRU_CONV = 4
LRU_C = 8.0
LRU_IN = 512

RET_HEADS = 4
RET_QK_DIM = 32
RET_V_DIM = 64
RET_WIDTH = 256
RET_QK = RET_HEADS * RET_QK_DIM
RET_IN = 768
RET_GN_EPS = 1e-5
ROPE_BASE = 10000.0

D_FF = 2816

VMEM_LIMIT_BYTES = 52 * 1024 * 1024

P_SSD_W = SSD_WIDTH + SSD_CONV_CH + SSD_WIDTH
IN_COLS = P_SSD_W + RWKV_IN + LRU_IN + RET_IN


def _mm(a, b):
    return jnp.dot(a.astype(BF16), b.astype(BF16), preferred_element_type=F32)


def _mm_nt(a, b):
    return lax.dot_general(a.astype(BF16), b.astype(BF16), (((1,), (1,)), ((), ())),
                           preferred_element_type=F32)


def _mm_tn(a, b):
    return lax.dot_general(a.astype(BF16), b.astype(BF16), (((0,), (0,)), ((), ())),
                           preferred_element_type=F32)


def _mm_exact(a, b):
    return jnp.dot(a, b, preferred_element_type=F32, precision=lax.Precision.HIGHEST)


def _sigmoid(x):
    return 1.0 / (1.0 + jnp.exp(-x))


def _silu(x):
    return x * _sigmoid(x)


def _softplus(x):
    return jnp.maximum(x, 0.0) + jnp.log1p(jnp.exp(-jnp.abs(x)))


def _rms(x, w):
    return x * lax.rsqrt(jnp.mean(x * x, axis=-1, keepdims=True) + NORM_EPS) * w


def _causal_conv(buf, w, b, rows):
    acc = b + w[3:4, :] * buf[8:8 + rows, :]
    acc = acc + w[2:3, :] * buf[7:7 + rows, :]
    acc = acc + w[1:2, :] * buf[6:6 + rows, :]
    acc = acc + w[0:1, :] * buf[5:5 + rows, :]
    return acc


def _prep_w_in_kernel(w_ref, o_ref):
    main = SSD_WIDTH + SSD_CONV_CH
    o_ref[:, 0:main] = w_ref[:, 0:main].astype(BF16)
    lane_lo = lax.broadcasted_iota(jnp.int32, (1, 128), 1) < SSD_HEAD_DIM
    for g in range(SSD_HEADS // 2):
        c0 = w_ref[:, main + 2 * g:main + 2 * g + 1]
        c1 = w_ref[:, main + 2 * g + 1:main + 2 * g + 2]
        o_ref[:, main + 128 * g:main + 128 * g + 128] = jnp.where(lane_lo, c0, c1).astype(BF16)
    o_ref[:, P_SSD_W:] = w_ref[:, SSD_IN:].astype(BF16)


def _prep_w_in(w_in):
    depth, d, n = w_in.shape
    tr = 256
    return pl.pallas_call(
        _prep_w_in_kernel,
        grid=(depth, d // tr),
        in_specs=[pl.BlockSpec((None, tr, n), lambda l, i: (l, i, 0))],
        out_specs=pl.BlockSpec((None, tr, IN_COLS), lambda l, i: (l, i, 0)),
        out_shape=jax.ShapeDtypeStruct((depth, d, IN_COLS), BF16),
        compiler_params=pltpu.CompilerParams(dimension_semantics=("parallel", "parallel")),
        name="prep_w_in",
    )(w_in)


def _in_proj_kernel(h_ref, nw_ref, w_ref, ssd_ref, rwkv_ref, lru_ref, ret_ref):
    tm = h_ref.shape[0]
    row = pl.program_id(1) * tm + lax.broadcasted_iota(jnp.int32, (tm, 1), 0)
    hn = _rms(h_ref[...], nw_ref[...])
    hn = jnp.where(row >= PAD, hn, 0.0).astype(BF16)
    o = 0
    for ref in (ssd_ref, rwkv_ref, lru_ref, ret_ref):
        n = ref.shape[1]
        ref[...] = jnp.dot(hn, w_ref[:, o:o + n], preferred_element_type=F32)
        o += n


def _in_proj(h, nw, w, layer, tm):
    bsz, tp, d = h.shape
    widths = (P_SSD_W, RWKV_IN, LRU_IN, RET_IN)
    return pl.pallas_call(
        _in_proj_kernel,
        grid=(bsz, tp // tm),
        in_specs=[pl.BlockSpec((None, tm, d), lambda b, i: (b, i, 0)),
                  pl.BlockSpec((1, d), lambda b, i: (0, 0)),
                  pl.BlockSpec((None, d, IN_COLS), lambda b, i: (layer, 0, 0))],
        out_specs=[pl.BlockSpec((None, tm, n), lambda b, i: (b, i, 0)) for n in widths],
        out_shape=[jax.ShapeDtypeStruct((bsz, tp, n), F32) for n in widths],
        compiler_params=pltpu.CompilerParams(
            dimension_semantics=("parallel", "arbitrary"), vmem_limit_bytes=VMEM_LIMIT_BYTES),
        name="in_proj",
    )(h, nw, w)


def _split_bf16(x, parts):
    out = []
    for _ in range(parts):
        hi = x.astype(BF16)
        out.append(hi)
        x = x - hi.astype(F32)
    return out


def _cumsum_rows(x, tril3_ref):
    return jnp.dot(tril3_ref[...], jnp.concatenate(_split_bf16(x, 3), axis=0), preferred_element_type=F32)


def _tril3(n):
    tril = (np.arange(n)[:, None] >= np.arange(n)[None, :]).astype(np.float32)
    return jnp.asarray(np.concatenate([tril] * 3, axis=1), BF16)


def _ssd_kernel(p_ref, cw_ref, cb_ref, dtb_ref, alog_ref, dsk_ref, nw_ref, tril3_ref, y_ref, xbuf, state):
    c = pl.program_id(1)
    nb = p_ref.shape[0]
    L = CHUNK

    @pl.when(c == 0)
    def _():
        xbuf[:, 0:8, :] = jnp.zeros((nb, 8, SSD_CONV_CH), F32)
        state[...] = jnp.zeros(state.shape, F32)

    row = c * L + lax.broadcasted_iota(jnp.int32, (L, 1), 0)
    causal = lax.broadcasted_iota(jnp.int32, (L, L), 0) >= lax.broadcasted_iota(jnp.int32, (L, L), 1)
    lane_lo = lax.broadcasted_iota(jnp.int32, (1, 128), 1) < SSD_HEAD_DIM
    groups = [slice(128 * g, 128 * g + 128) for g in range(2)]

    def prep(b):
        buf = xbuf.at[b]
        buf[8:8 + L, :] = p_ref[b, :, SSD_WIDTH:SSD_WIDTH + SSD_CONV_CH]
        xbc = _silu(_causal_conv(buf, cw_ref[...], cb_ref[...], L))
        buf[0:8, :] = buf[L:L + 8, :]
        dt = _softplus(p_ref[b, :, SSD_WIDTH + SSD_CONV_CH:P_SSD_W] + dtb_ref[...])
        dt = jnp.where(row >= PAD, dt, 0.0)
        a = dt * (-jnp.exp(alog_ref[...]))
        xs = xbc[:, 0:256]
        return dict(xs=xs, bs=xbc[:, 256:512], cs=xbc[:, 512:768], xdt=xs * dt,
                    acs=_cumsum_rows(a, tril3_ref), y=[None, None])

    def gram(d, g):
        sl = groups[g]
        return _mm_nt(d["cs"][:, sl], d["bs"][:, sl])

    def chunk(b, d, g, gmat):
        sl = groups[g]
        acs_g = d["acs"][:, sl]
        acs_t = acs_g.T
        a_last = acs_g[L - 1:L, :]
        b_g, c_g, x_g = d["bs"][:, sl], d["cs"][:, sl], d["xdt"][:, sl]
        ms = []
        for hh in range(2):
            o = SSD_HEAD_DIM * hh
            diff = acs_g[:, o:o + 1] - acs_t[o:o + 1, :]
            ms.append(gmat * jnp.exp(jnp.where(causal, diff, -1e30)))
        mcat = jnp.concatenate(ms, axis=1)
        xbd = jnp.concatenate([jnp.where(lane_lo, x_g, 0.0), jnp.where(lane_lo, 0.0, x_g)], axis=0)
        s_in = state[b, g]
        d["y"][g] = _mm(mcat, xbd) + _mm(c_g, s_in) * jnp.exp(acs_g)
        state[b, g] = s_in * jnp.exp(a_last) + _mm_tn(b_g, x_g * jnp.exp(a_last - acs_g))

    def finish(b, d, g):
        sl = groups[g]
        y_g = (d["y"][g] + dsk_ref[:, sl] * d["xs"][:, sl]) * _silu(p_ref[b, :, sl])
        y_ref[b, :, sl] = _rms(y_g, nw_ref[:, sl])

    ds = [prep(b) for b in range(nb)]
    gm = [[gram(d, g) for g in range(2)] for d in ds]
    for b, d in enumerate(ds):
        for g in range(2):
            chunk(b, d, g, gm[b][g])
    for b, d in enumerate(ds):
        for g in range(2):
            finish(b, d, g)


def _ssd_mix(p, cw, cb, dtb, alog, dsk, nw, nb):
    bsz, tp, _ = p.shape
    vec = lambda n: pl.BlockSpec((1, n), lambda b, c: (0, 0))
    return pl.pallas_call(
        _ssd_kernel,
        grid=(bsz // nb, tp // CHUNK),
        in_specs=[pl.BlockSpec((nb, CHUNK, P_SSD_W), lambda b, c: (b, c, 0)),
                  pl.BlockSpec((SSD_CONV, SSD_CONV_CH), lambda b, c: (0, 0)),
                  vec(SSD_CONV_CH), vec(SSD_WIDTH), vec(SSD_WIDTH), vec(SSD_WIDTH), vec(SSD_WIDTH),
                  pl.BlockSpec((CHUNK, 3 * CHUNK), lambda b, c: (0, 0))],
        out_specs=pl.BlockSpec((nb, CHUNK, SSD_WIDTH), lambda b, c: (b, c, 0)),
        out_shape=jax.ShapeDtypeStruct((bsz, tp, SSD_WIDTH), F32),
        scratch_shapes=[pltpu.VMEM((nb, CHUNK + 8, SSD_CONV_CH), F32),
                        pltpu.VMEM((nb, 2, SSD_STATE, 128), F32)],
        compiler_params=pltpu.CompilerParams(dimension_semantics=("parallel", "arbitrary")),
        name="ssd_mix",
    )(p, cw, cb, dtb, alog, dsk, nw, _tril3(CHUNK))


def _seg_sum(x, seg2_ref):
    return jnp.dot(jnp.concatenate(_split_bf16(x, 2), axis=1), seg2_ref[...], preferred_element_type=F32)


def _rwkv_kernel(p_ref, mu_ref, w0_ref, a0_ref, wa2_ref, g2_ref, kk_ref, ka_ref, rk_ref,
                 lnw_ref, lnb_ref, seg2_ref, tril3_ref, mask_ref, maskw_ref, y_ref, pbuf, state):
    c = pl.program_id(1)
    nb = p_ref.shape[0]
    C = RWKV_CHUNK
    W = RWKV_WIDTH
    R = RWKV_HEADS * C

    @pl.when(c == 0)
    def _():
        pbuf[:, 0:8, :] = jnp.zeros((nb, 8, RWKV_IN), F32)
        state[...] = jnp.zeros(state.shape, F32)

    lane = lax.broadcasted_iota(jnp.int32, (1, 128), 1)
    lane_head = lax.broadcasted_iota(jnp.int32, (1, W), 1) // RWKV_HEAD_DIM

    def stack(x):
        x = x.astype(BF16)
        zero = jnp.zeros_like(x)
        return jnp.concatenate([jnp.where(lane_head == h, x, zero) for h in range(RWKV_HEADS)], axis=0)

    def tile(x):
        return jnp.concatenate([x] * RWKV_HEADS, axis=0)

    def prep(b):
        pbuf[b, 8:8 + C, :] = p_ref[b]
        p = pbuf[b, 8:8 + C, :]
        p = p + (pbuf[b, 7:7 + C, :] - p) * mu_ref[...]
        pbuf[b, 0:8, :] = pbuf[b, C:C + 8, :]
        r, k, v = p[:, 0:W], p[:, W:2 * W], p[:, 2 * W:3 * W]
        lat = p[:, 3 * W:3 * W + 128]
        lat = jnp.where(lane < 64, jnp.tanh(lat), lat)
        wa = _mm(lat, wa2_ref[...])
        w = -_softplus(-(w0_ref[...] + wa[:, 0:W])) - 0.5
        logw = -jnp.exp(w)
        a = _sigmoid(a0_ref[...] + wa[:, W:2 * W])
        g = _mm(_sigmoid(p[:, 3 * W + 128:3 * W + 256]), g2_ref[...])
        kk = k * kk_ref[...]
        kk = kk / jnp.maximum(jnp.sqrt(_seg_sum(kk * kk, seg2_ref)), 1e-12)
        k2 = k * (1.0 + (a - 1.0) * ka_ref[...])
        bonus = _seg_sum(r * k2 * rk_ref[...], seg2_ref) * v
        cum = _cumsum_rows(logw, tril3_ref)
        e_neg = jnp.exp(-cum)
        pc = jnp.exp(cum[C - 1:C, :])
        bt = kk * a * e_neg
        kt = k2 * e_neg
        return dict(
            lhs=jnp.concatenate([-kk * jnp.exp(cum - logw), r * jnp.exp(cum)], axis=0).astype(BF16),
            rhs=jnp.concatenate([stack(bt), stack(kt)], axis=0),
            upd=jnp.concatenate([bt * pc, kt * pc], axis=0).astype(BF16),
            v=v.astype(BF16), v_bd=stack(v), pc=pc, bonus=bonus, g=g)

    def scores(d):
        sc = lax.dot_general(d["lhs"], d["rhs"], (((1,), (1,)), ((), ())), preferred_element_type=F32)
        strict_w, incl_w = maskw_ref[0], maskw_ref[1]
        d["npow"] = tile(sc[0:C, 0:R]) * mask_ref[0]
        d["a_k"] = jnp.concatenate([sc[0:C, R:2 * R] * strict_w, sc[C:2 * C, R:2 * R] * incl_w],
                                   axis=0).astype(BF16)
        d["a_rb"] = (sc[C:2 * C, 0:R] * incl_w).astype(BF16)
        d["tinv"] = mask_ref[1] + d["npow"]

    def double(d):
        n16 = d["npow"].astype(BF16)
        d["npow"] = jnp.dot(n16, n16, preferred_element_type=F32)
        d["tinv"] = d["tinv"] + _mm(d["tinv"], d["npow"])

    def read_state(b, d):
        d["sxav"] = _mm_nt(d["lhs"], state[b]) + jnp.dot(d["a_k"], d["v_bd"], preferred_element_type=F32)

    def correction(d):
        d["u"] = _mm(d["tinv"], stack(d["sxav"][0:C]))

    def write_state(b, d):
        u = d["u"]
        d["y"] = d["sxav"][C:2 * C] + jnp.dot(d["a_rb"], u.astype(BF16), preferred_element_type=F32)
        u_w = (u[0:C] + u[C:2 * C] + u[2 * C:3 * C] + u[3 * C:4 * C]).astype(BF16)
        new = _mm_tn(jnp.concatenate([u_w, d["v"]], axis=0), d["upd"])
        state[b] = state[b] * d["pc"] + new * mask_ref[2]

    def finish(b, d):
        inv_n = 1.0 / RWKV_HEAD_DIM
        yc = d["y"] - _seg_sum(d["y"], seg2_ref) * inv_n
        var = _seg_sum(yc * yc, seg2_ref) * inv_n
        yn = yc * lax.rsqrt(var + RWKV_GN_EPS) * lnw_ref[...] + lnb_ref[...]
        y_ref[b] = (yn + d["bonus"]) * d["g"]

    ds = [prep(b) for b in range(nb)]
    for d in ds:
        scores(d)
    n = 1
    while 2 * n < C:
        for d in ds:
            double(d)
        n *= 2
    for b, d in enumerate(ds):
        read_state(b, d)
    for d in ds:
        correction(d)
    for b, d in enumerate(ds):
        write_state(b, d)
    for b, d in enumerate(ds):
        finish(b, d)


def _rwkv_tables():
    C, W, R = RWKV_CHUNK, RWKV_WIDTH, RWKV_HEADS * RWKV_CHUNK
    head = np.arange(W) // RWKV_HEAD_DIM
    seg = (head[:, None] == head[None, :]).astype(np.float32)
    assert R == W
    rr, cc = np.arange(R)[:, None], np.arange(R)[None, :]
    strict = ((rr // C) == (cc // C)) & ((cc % C) < (rr % C))
    tt = np.arange(C)[:, None]
    wide = np.stack([(cc % C) < tt, (cc % C) <= tt])
    return (jnp.asarray(np.concatenate([seg, seg], axis=0), BF16), _tril3(C),
            jnp.asarray(np.stack([strict, rr == cc, seg > 0]).astype(np.float32), F32),
            jnp.asarray(wide.astype(np.float32), F32))


def _rwkv_mix(p, mu, w0, a0, wa2, g2, kk, ka, rk, lnw, lnb, nb):
    bsz, tp, _ = p.shape
    C, W, R = RWKV_CHUNK, RWKV_WIDTH, RWKV_HEADS * RWKV_CHUNK
    vec = lambda n: pl.BlockSpec((1, n), lambda b, c: (0, 0))
    full = lambda s: pl.BlockSpec(s, lambda b, c: (0,) * len(s))
    return pl.pallas_call(
        _rwkv_kernel,
        grid=(bsz // nb, tp // C),
        in_specs=[pl.BlockSpec((nb, C, RWKV_IN), lambda b, c: (b, c, 0)),
                  vec(RWKV_IN), vec(W), vec(W), full((128, 2 * W)), full((128, W)),
                  vec(W), vec(W), vec(W), vec(W), vec(W),
                  full((2 * W, W)), full((C, 3 * C)), full((3, R, R)), full((2, C, R))],
        out_specs=pl.BlockSpec((nb, C, W), lambda b, c: (b, c, 0)),
        out_shape=jax.ShapeDtypeStruct((bsz, tp, W), F32),
        scratch_shapes=[pltpu.VMEM((nb, C + 8, RWKV_IN), F32),
                        pltpu.VMEM((nb, W, W), F32)],
        compiler_params=pltpu.CompilerParams(dimension_semantics=("parallel", "arbitrary")),
        name="rwkv_mix",
    )(p, mu, w0, a0, wa2, g2, kk, ka, rk, lnw, lnb, *_rwkv_tables())


def _lru_kernel(p_ref, cw_ref, cb_ref, wax_ref, bax_ref, lam_ref, y_ref, xbuf, hprev):
    c = pl.program_id(1)
    nb = p_ref.shape[0]
    L = CHUNK
    W = LRU_WIDTH

    @pl.when(c == 0)
    def _():
        xbuf[:, 0:8, :] = jnp.zeros((nb, 8, W), F32)
        hprev[...] = jnp.zeros(hprev.shape, F32)

    rowi = lax.broadcasted_iota(jnp.int32, (L, 1), 0)
    log_a_unit = -LRU_C * _softplus(-lam_ref[...])
    for b in range(nb):
        buf = xbuf.at[b]
        buf[8:8 + L, :] = p_ref[b, :, 0:W]
        xc = _causal_conv(buf, cw_ref[...], cb_ref[...], L)
        buf[0:8, :] = buf[L:L + 8, :]
        gates = _sigmoid(_mm(xc, wax_ref[...]) + bax_ref[...])
        r, i = gates[:, 0:W], gates[:, W:2 * W]
        log_a = r * log_a_unit
        a = jnp.exp(log_a)
        t = jnp.tanh(log_a)
        one_minus_a2 = -2.0 * t / (1.0 - t)
        u = jnp.sqrt(one_minus_a2) * (i * xc)
        u = jnp.where(c * L + rowi >= PAD, u, 0.0)

        s = 1
        while s < 8:
            keep = (rowi & 7) >= s
            a_sh = jnp.where(keep, pltpu.roll(a, s, axis=0), 1.0)
            u_sh = jnp.where(keep, pltpu.roll(u, s, axis=0), 0.0)
            u = u + a * u_sh
            a = a * a_sh
            s *= 2
        carry = hprev[b, 0:1, :]
        groups = []
        for i in range(L // 8):
            h_i = u[8 * i:8 * i + 8, :] + a[8 * i:8 * i + 8, :] * carry
            groups.append(h_i)
            carry = h_i[7:8, :]
        hprev[b, 0:1, :] = carry
        h = jnp.concatenate(groups, axis=0)
        y_ref[b] = h * jax.nn.gelu(p_ref[b, :, W:2 * W], approximate=True)


def _lru_mix(p, cw, cb, wax, bax, lam, nb):
    bsz, tp, _ = p.shape
    vec = lambda n: pl.BlockSpec((1, n), lambda b, c: (0, 0))
    return pl.pallas_call(
        _lru_kernel,
        grid=(bsz // nb, tp // CHUNK),
        in_specs=[pl.BlockSpec((nb, CHUNK, LRU_IN), lambda b, c: (b, c, 0)),
                  pl.BlockSpec((LRU_CONV, LRU_WIDTH), lambda b, c: (0, 0)),
                  vec(LRU_WIDTH),
                  pl.BlockSpec((LRU_WIDTH, 2 * LRU_WIDTH), lambda b, c: (0, 0)),
                  vec(2 * LRU_WIDTH), vec(LRU_WIDTH)],
        out_specs=pl.BlockSpec((nb, CHUNK, LRU_WIDTH), lambda b, c: (b, c, 0)),
        out_shape=jax.ShapeDtypeStruct((bsz, tp, LRU_WIDTH), F32),
        scratch_shapes=[pltpu.VMEM((nb, CHUNK + 8, LRU_WIDTH), F32),
                        pltpu.VMEM((nb, 8, LRU_WIDTH), F32)],
        compiler_params=pltpu.CompilerParams(dimension_semantics=("parallel", "arbitrary")),
        name="lru_mix",
    )(p, cw, cb, wax, bax, lam)


def _ret_kernel(p_ref, cos_ref, sin_ref, dmat_ref, kdec_ref, qdec_ref, cross_ref, bd_ref, seg2_ref,
                gnw_ref, y_ref, state):
    c = pl.program_id(1)
    nb = p_ref.shape[0]
    W = RET_WIDTH

    @pl.when(c == 0)
    def _():
        state[...] = jnp.zeros(state.shape, F32)

    lane = lax.broadcasted_iota(jnp.int32, (1, RET_QK), 1)
    first_half = (lane % RET_QK_DIM) < (RET_QK_DIM // 2)
    half = RET_QK_DIM // 2
    qk_head = lane // RET_QK_DIM
    v_head = lax.broadcasted_iota(jnp.int32, (1, W), 1) // RET_V_DIM

    def rope(x):
        swapped = jnp.where(first_half, pltpu.roll(x, RET_QK - half, axis=1), pltpu.roll(x, half, axis=1))
        return x * cos_ref[...] + swapped * sin_ref[...]

    def prep(b):
        q = rope(p_ref[b, :, 0:RET_QK])
        k = rope(p_ref[b, :, RET_QK:2 * RET_QK]) * (RET_QK_DIM ** -0.5)
        v = p_ref[b, :, 2 * RET_QK:2 * RET_QK + W]
        return dict(q=q, k32=k, k=k.astype(BF16), v=v.astype(BF16), v32=v)

    def scores(d):
        d["s"] = [(_mm_nt(jnp.where(qk_head == h, d["q"], 0.0), d["k"]) * dmat_ref[h]).astype(BF16)
                  for h in range(RET_HEADS)]

    def outputs(b, d):
        y = _mm(d["q"] * qdec_ref[...], state[b])
        for h in range(RET_HEADS):
            y = y + jnp.dot(d["s"][h], jnp.where(v_head == h, d["v32"], 0.0).astype(BF16),
                            preferred_element_type=F32)
        d["y"] = y
        kv = _mm_tn(d["k32"] * kdec_ref[...], d["v"]) * bd_ref[...]
        state[b] = state[b] * cross_ref[...] + kv

    def center(d):
        d["yc"] = d["y"] - _seg_sum(d["y"], seg2_ref) * (1.0 / RET_V_DIM)

    def finish(b, d):
        var = _seg_sum(d["yc"] * d["yc"], seg2_ref) * (1.0 / RET_V_DIM)
        g = p_ref[b, :, 2 * RET_QK + W:2 * RET_QK + 2 * W]
        y_ref[b] = d["yc"] * lax.rsqrt(var + RET_GN_EPS) * gnw_ref[...] * _silu(g)

    ds = [prep(b) for b in range(nb)]
    for d in ds:
        scores(d)
    for b, d in enumerate(ds):
        outputs(b, d)
    for d in ds:
        center(d)
    for b, d in enumerate(ds):
        finish(b, d)


def _ret_mix(p, cos_t, sin_t, dmat, kdec, qdec, cross, bd, seg2, gnw, nb):
    bsz, tp, _ = p.shape
    const2 = lambda s: pl.BlockSpec(s, lambda b, c: (0, 0))
    return pl.pallas_call(
        _ret_kernel,
        grid=(bsz // nb, tp // CHUNK),
        in_specs=[pl.BlockSpec((nb, CHUNK, RET_IN), lambda b, c: (b, c, 0)),
                  pl.BlockSpec((CHUNK, RET_QK), lambda b, c: (c, 0)),
                  pl.BlockSpec((CHUNK, RET_QK), lambda b, c: (c, 0)),
                  pl.BlockSpec((RET_HEADS, CHUNK, CHUNK), lambda b, c: (0, 0, 0)),
                  const2((CHUNK, RET_QK)), const2((CHUNK, RET_QK)),
                  const2((RET_QK, RET_WIDTH)), const2((RET_QK, RET_WIDTH)),
                  const2((2 * RET_WIDTH, RET_WIDTH)), const2((1, RET_WIDTH))],
        out_specs=pl.BlockSpec((nb, CHUNK, RET_WIDTH), lambda b, c: (b, c, 0)),
        out_shape=jax.ShapeDtypeStruct((bsz, tp, RET_WIDTH), F32),
        scratch_shapes=[pltpu.VMEM((nb, RET_QK, RET_WIDTH), F32)],
        compiler_params=pltpu.CompilerParams(dimension_semantics=("parallel", "arbitrary")),
        name="ret_mix",
    )(p, cos_t, sin_t, dmat, kdec, qdec, cross, bd, seg2, gnw)


def _retention_tables(tp):
    half = RET_QK_DIM // 2
    lane = np.arange(RET_QK)
    freqs = ROPE_BASE ** (-np.arange(half, dtype=np.float64) / half)
    pos = np.arange(tp, dtype=np.float64) - PAD
    ang = pos[:, None] * freqs[lane % half][None, :]
    sign = np.where((lane % RET_QK_DIM) < half, -1.0, 1.0)
    cos_t = np.cos(ang)
    sin_t = np.sin(ang) * sign[None, :]
    log_g = np.log1p(-np.exp2(-5.0 - np.arange(RET_HEADS, dtype=np.float64)))
    idx = np.arange(CHUNK)
    rel = idx[:, None] - idx[None, :]
    dmat = np.where(rel >= 0, np.exp(np.maximum(rel, 0)[None] * log_g[:, None, None]), 0.0)
    lg_lane = log_g[lane // RET_QK_DIM]
    kdec = np.exp((CHUNK - 1 - idx)[:, None] * lg_lane[None, :])
    qdec = np.exp((idx + 1)[:, None] * lg_lane[None, :])
    cross = np.broadcast_to(np.exp(CHUNK * lg_lane)[:, None], (RET_QK, RET_WIDTH))
    v_head = np.arange(RET_WIDTH) // RET_V_DIM
    bd = (lane // RET_QK_DIM)[:, None] == v_head[None, :]
    seg = (v_head[:, None] == v_head[None, :]).astype(np.float32)
    f = lambda x: jnp.asarray(np.ascontiguousarray(x), F32)
    return (f(cos_t), f(sin_t), f(dmat), f(kdec), f(qdec), f(cross), f(bd),
            jnp.asarray(np.concatenate([seg, seg], axis=0), BF16))


def _out_ffn_kernel(h_ref, y0_ref, y1_ref, y2_ref, y3_ref, wo_ref, n1_ref, n2_ref, n3_ref,
                    wg_ref, wu_ref, wd_ref, o_ref, *, tf):
    u = None
    for i, yr in enumerate((y0_ref, y1_ref, y2_ref, y3_ref)):
        part = _mm(yr[...], wo_ref[256 * i:256 * i + 256, :])
        u = part if u is None else u + part
    h1 = h_ref[...] + _rms(u, n1_ref[...])
    hn = _rms(h1, n2_ref[...]).astype(BF16)
    acc = None
    for j in range(D_FF // tf):
        sl = slice(j * tf, (j + 1) * tf)
        act = _silu(jnp.dot(hn, wg_ref[:, sl], preferred_element_type=F32)) * \
            jnp.dot(hn, wu_ref[:, sl], preferred_element_type=F32)
        part = jnp.dot(act.astype(BF16), wd_ref[sl, :], preferred_element_type=F32)
        acc = part if acc is None else acc + part
    o_ref[...] = h1 + _rms(acc, n3_ref[...])


def _out_ffn(h, ys, wo, n1, n2, n3, wg, wu, wd, layer, tm, tf):
    m, d = h.shape
    row = lambda n: pl.BlockSpec((tm, n), lambda i: (i, 0))
    full = lambda s: pl.BlockSpec(s, lambda i: (0, 0))
    stacked = lambda s: pl.BlockSpec((None,) + s, lambda i: (layer, 0, 0))
    return pl.pallas_call(
        functools.partial(_out_ffn_kernel, tf=tf),
        grid=(m // tm,),
        in_specs=[row(d), row(256), row(256), row(256), row(256),
                  stacked((d, d)), full((1, d)), full((1, d)), full((1, d)),
                  stacked((d, D_FF)), stacked((d, D_FF)), stacked((D_FF, d))],
        out_specs=row(d),
        out_shape=jax.ShapeDtypeStruct((m, d), F32),
        compiler_params=pltpu.CompilerParams(
            dimension_semantics=("parallel",), vmem_limit_bytes=VMEM_LIMIT_BYTES),
        name="out_ffn",
    )(h, *ys, wo, n1, n2, n3, wg, wu, wd)


def _block_diag(blocks):
    g, n, m = blocks.shape
    eye = jnp.eye(g, dtype=blocks.dtype)
    return (eye[:, None, :, None] * blocks[:, :, None, :]).reshape(g * n, g * m)


def _row_tile(tp):
    best = 8
    for t in range(8, 641, 8):
        if tp % t == 0:
            best = t
    return best


def kernel(x, meta_tokens, pre_mix_norm, post_mix_norm, pre_ffn_norm, post_ffn_norm, w_in, w_out, ssd_conv_w, ssd_conv_b, ssd_dt_bias, ssd_a_log, ssd_d, ssd_norm_w, rwkv_mu, rwkv_w0, rwkv_w2, rwkv_a0, rwkv_a2, rwkv_g2, rwkv_k_k, rwkv_k_a, rwkv_r_k, rwkv_ln_w, rwkv_ln_b, lru_conv_w, lru_conv_b, lru_wa, lru_ba, lru_wx, lru_bx, lru_lambda, ret_gn_w, ffn_w_gate, ffn_w_up, ffn_w_down):
    bsz, seq, d = x.shape
    depth = w_in.shape[0]
    t = N_META + seq
    tp = t + PAD
    assert d == D_MODEL and tp % CHUNK == 0
    meta = jnp.broadcast_to(meta_tokens.astype(x.dtype)[None], (bsz, N_META, d))
    h = jnp.concatenate([jnp.zeros((bsz, PAD, d), x.dtype), meta, x], axis=1)

    tm_in = _row_tile(tp)
    m_rows = bsz * tp
    tm_ffn = 512 if m_rows % 512 == 0 else CHUNK
    tf = 256
    tables = _retention_tables(tp)
    per_head = lambda v: jnp.repeat(v, SSD_HEAD_DIM, axis=-1)[None]
    r2 = lambda v: v[None]

    w_cat = _prep_w_in(w_in)
    nb = next(n for n in (8, 4, 2, 1) if bsz % n == 0)
    nb_rwkv = nb
    wo16, wg16, wu16, wd16 = (w.astype(BF16) for w in (w_out, ffn_w_gate, ffn_w_up, ffn_w_down))

    for l in range(depth):
        p_ssd, p_rwkv, p_lru, p_ret = _in_proj(h, r2(pre_mix_norm[l]), w_cat, l, tm_in)

        y_ssd = _ssd_mix(p_ssd, ssd_conv_w[l], r2(ssd_conv_b[l]), per_head(ssd_dt_bias[l]),
                         per_head(ssd_a_log[l]), per_head(ssd_d[l]), r2(ssd_norm_w[l]), nb)

        zero = jnp.zeros((64, RWKV_WIDTH), F32)
        wa2 = jnp.concatenate([jnp.concatenate([rwkv_w2[l], zero], axis=1),
                               jnp.concatenate([zero, rwkv_a2[l]], axis=1)], axis=0).astype(BF16)
        y_rwkv = _rwkv_mix(p_rwkv, r2(rwkv_mu[l]), r2(rwkv_w0[l]), r2(rwkv_a0[l]), wa2,
                           rwkv_g2[l].astype(BF16), r2(rwkv_k_k[l]), r2(rwkv_k_a[l]),
                           rwkv_r_k[l].reshape(1, RWKV_WIDTH), r2(rwkv_ln_w[l]), r2(rwkv_ln_b[l]),
                           nb_rwkv)

        wax = jnp.concatenate([_block_diag(lru_wa[l]), _block_diag(lru_wx[l])], axis=1).astype(BF16)
        bax = jnp.concatenate([lru_ba[l], lru_bx[l]])[None]
        y_lru = _lru_mix(p_lru, lru_conv_w[l], r2(lru_conv_b[l]), wax, bax, r2(lru_lambda[l]), nb)

        y_ret = _ret_mix(p_ret, *tables, r2(ret_gn_w[l]), nb)

        flat = lambda a: a.reshape(m_rows, a.shape[-1])
        h = _out_ffn(flat(h), [flat(y_ssd), flat(y_rwkv), flat(y_lru), flat(y_ret)],
                     wo16, r2(post_mix_norm[l]), r2(pre_ffn_norm[l]), r2(post_ffn_norm[l]),
                     wg16, wu16, wd16, l, tm_ffn, tf).reshape(bsz, tp, d)
    return h[:, PAD + N_META:]
```

```python
import functools
import math

import numpy as np
import jax
import jax.numpy as jnp
from jax import lax
from jax.experimental import pallas as pl
from jax.experimental.pallas import tpu as pltpu

F32 = jnp.float32
BF16 = jnp.bfloat16

D_MODEL = 1024
N_META = 16
CHUNK = 128
PAD = CHUNK - N_META
NORM_EPS = 1e-6

SSD_HEADS = 4
SSD_HEAD_DIM = 64
SSD_WIDTH = 256
SSD_STATE = 128
SSD_CONV = 4
SSD_CONV_CH = 768
SSD_IN = 1028

RWKV_HEADS = 4
RWKV_HEAD_DIM = 64
RWKV_WIDTH = 256
RWKV_IN = 1024
RWKV_GN_EPS = 64e-5
RWKV_CHUNK = 64

LRU_WIDTH = 256
LRU_CONV = 4
LRU_C = 8.0
LRU_IN = 512

RET_HEADS = 4
RET_QK_DIM = 32
RET_V_DIM = 64
RET_WIDTH = 256
RET_QK = RET_HEADS * RET_QK_DIM
RET_IN = 768
RET_GN_EPS = 1e-5
ROPE_BASE = 10000.0

D_FF = 2816

VMEM_LIMIT_BYTES = 52 * 1024 * 1024

P_SSD_W = SSD_WIDTH + SSD_CONV_CH + SSD_WIDTH
IN_COLS = P_SSD_W + RWKV_IN + LRU_IN + RET_IN


def _mm(a, b):
    return jnp.dot(a.astype(BF16), b.astype(BF16), preferred_element_type=F32)


def _mm_nt(a, b):
    return lax.dot_general(a.astype(BF16), b.astype(BF16), (((1,), (1,)), ((), ())),
                           preferred_element_type=F32)


def _mm_tn(a, b):
    return lax.dot_general(a.astype(BF16), b.astype(BF16), (((0,), (0,)), ((), ())),
                           preferred_element_type=F32)


def _mm_exact(a, b):
    return jnp.dot(a, b, preferred_element_type=F32, precision=lax.Precision.HIGHEST)


def _sigmoid(x):
    return 1.0 / (1.0 + jnp.exp(-x))


def _silu(x):
    return x * _sigmoid(x)


def _softplus(x):
    return jnp.maximum(x, 0.0) + jnp.log1p(jnp.exp(-jnp.abs(x)))


def _rms(x, w):
    return x * lax.rsqrt(jnp.mean(x * x, axis=-1, keepdims=True) + NORM_EPS) * w


def _causal_conv(buf, w, b, rows):
    acc = b + w[3:4, :] * buf[8:8 + rows, :]
    acc = acc + w[2:3, :] * buf[7:7 + rows, :]
    acc = acc + w[1:2, :] * buf[6:6 + rows, :]
    acc = acc + w[0:1, :] * buf[5:5 + rows, :]
    return acc


def _prep_w_in_kernel(w_ref, dt_ref, o_ref):
    main = SSD_WIDTH + SSD_CONV_CH
    o_ref[:, 0:main] = w_ref[:, 0:main]
    o_ref[:, main:P_SSD_W] = dt_ref[...]
    o_ref[:, P_SSD_W:] = w_ref[:, SSD_IN:]


def _prep_w_in(w_in):
    depth, d, n = w_in.shape
    tr = 256
    main = SSD_WIDTH + SSD_CONV_CH
    w16 = w_in.astype(BF16)
    dt_rep = jnp.repeat(w_in[:, :, main:SSD_IN], SSD_HEAD_DIM, axis=2).astype(BF16)
    return pl.pallas_call(
        _prep_w_in_kernel,
        grid=(depth, d // tr),
        in_specs=[pl.BlockSpec((None, tr, n), lambda l, i: (l, i, 0)),
                  pl.BlockSpec((None, tr, SSD_WIDTH), lambda l, i: (l, i, 0))],
        out_specs=pl.BlockSpec((None, tr, IN_COLS), lambda l, i: (l, i, 0)),
        out_shape=jax.ShapeDtypeStruct((depth, d, IN_COLS), BF16),
        compiler_params=pltpu.CompilerParams(dimension_semantics=("parallel", "parallel")),
        name="prep_w_in",
    )(w16, dt_rep)


def _in_proj_kernel(h_ref, nw_ref, w_ref, ssd_ref, rwkv_ref, lru_ref, ret_ref):
    tm = h_ref.shape[0]
    row = pl.program_id(1) * tm + lax.broadcasted_iota(jnp.int32, (tm, 1), 0)
    hn = _rms(h_ref[...], nw_ref[...])
    hn = jnp.where(row >= PAD, hn, 0.0).astype(BF16)
    o = 0
    for ref in (ssd_ref, rwkv_ref, lru_ref, ret_ref):
        n = ref.shape[1]
        ref[...] = jnp.dot(hn, w_ref[:, o:o + n], preferred_element_type=F32)
        o += n


def _in_proj(h, nw, w, layer, tm):
    bsz, tp, d = h.shape
    widths = (P_SSD_W, RWKV_IN, LRU_IN, RET_IN)
    return pl.pallas_call(
        _in_proj_kernel,
        grid=(bsz, tp // tm),
        in_specs=[pl.BlockSpec((None, tm, d), lambda b, i: (b, i, 0)),
                  pl.BlockSpec((1, d), lambda b, i: (0, 0)),
                  pl.BlockSpec((None, d, IN_COLS), lambda b, i: (layer, 0, 0))],
        out_specs=[pl.BlockSpec((None, tm, n), lambda b, i: (b, i, 0)) for n in widths],
        out_shape=[jax.ShapeDtypeStruct((bsz, tp, n), F32) for n in widths],
        compiler_params=pltpu.CompilerParams(
            dimension_semantics=("parallel", "arbitrary"), vmem_limit_bytes=VMEM_LIMIT_BYTES),
        name="in_proj",
    )(h, nw, w)


def _split_bf16(x, parts):
    out = []
    for _ in range(parts):
        hi = x.astype(BF16)
        out.append(hi)
        x = x - hi.astype(F32)
    return out


def _cumsum_rows(x, tril3_ref):
    return jnp.dot(tril3_ref[...], jnp.concatenate(_split_bf16(x, 3), axis=0), preferred_element_type=F32)


def _tril3(n):
    tril = (np.arange(n)[:, None] >= np.arange(n)[None, :]).astype(np.float32)
    return jnp.asarray(np.concatenate([tril] * 3, axis=1), BF16)


def _ssd_kernel(p_ref, cw_ref, cb_ref, dtb_ref, alog_ref, dsk_ref, nw_ref, tril3_ref, y_ref, xbuf, state):
    c = pl.program_id(1)
    nb = p_ref.shape[0]
    L = CHUNK

    @pl.when(c == 0)
    def _():
        xbuf[:, 0:8, :] = jnp.zeros((nb, 8, SSD_CONV_CH), F32)
        state[...] = jnp.zeros(state.shape, F32)

    row = c * L + lax.broadcasted_iota(jnp.int32, (L, 1), 0)
    causal = lax.broadcasted_iota(jnp.int32, (L, L), 0) >= lax.broadcasted_iota(jnp.int32, (L, L), 1)
    lane_lo = lax.broadcasted_iota(jnp.int32, (1, 128), 1) < SSD_HEAD_DIM
    groups = [slice(128 * g, 128 * g + 128) for g in range(2)]

    def prep(b):
        buf = xbuf.at[b]
        buf[8:8 + L, :] = p_ref[b, :, SSD_WIDTH:SSD_WIDTH + SSD_CONV_CH]
        xbc = _silu(_causal_conv(buf, cw_ref[...], cb_ref[...], L))
        buf[0:8, :] = buf[L:L + 8, :]
        dt = _softplus(p_ref[b, :, SSD_WIDTH + SSD_CONV_CH:P_SSD_W] + dtb_ref[...])
        dt = jnp.where(row >= PAD, dt, 0.0)
        a = dt * (-jnp.exp(alog_ref[...]))
        xs = xbc[:, 0:256]
        return dict(xs=xs, bs=xbc[:, 256:512], cs=xbc[:, 512:768], xdt=xs * dt,
                    acs=_cumsum_rows(a, tril3_ref), y=[None, None])

    def gram(d, g):
        sl = groups[g]
        return _mm_nt(d["cs"][:, sl], d["bs"][:, sl])

    def chunk(b, d, g, gmat):
        sl = groups[g]
        acs_g = d["acs"][:, sl]
        acs_t = acs_g.T
        a_last = acs_g[L - 1:L, :]
        b_g, c_g, x_g = d["bs"][:, sl], d["cs"][:, sl], d["xdt"][:, sl]
        ms = []
        for hh in range(2):
            o = SSD_HEAD_DIM * hh
            diff = acs_g[:, o:o + 1] - acs_t[o:o + 1, :]
            ms.append(gmat * jnp.exp(jnp.where(causal, diff, -1e30)))
        mcat = jnp.concatenate(ms, axis=1)
        xbd = jnp.concatenate([jnp.where(lane_lo, x_g, 0.0), jnp.where(lane_lo, 0.0, x_g)], axis=0)
        s_in = state[b, g]
        d["y"][g] = _mm(mcat, xbd) + _mm(c_g, s_in) * jnp.exp(acs_g)
        state[b, g] = s_in * jnp.exp(a_last) + _mm_tn(b_g, x_g * jnp.exp(a_last - acs_g))

    def finish(b, d, g):
        sl = groups[g]
        y_g = (d["y"][g] + dsk_ref[:, sl] * d["xs"][:, sl]) * _silu(p_ref[b, :, sl])
        y_ref[b, :, sl] = _rms(y_g, nw_ref[:, sl])

    ds = [prep(b) for b in range(nb)]
    gm = [[gram(d, g) for g in range(2)] for d in ds]
    for b, d in enumerate(ds):
        for g in range(2):
            chunk(b, d, g, gm[b][g])
    for b, d in enumerate(ds):
        for g in range(2):
            finish(b, d, g)


def _ssd_mix(p, cw, cb, dtb, alog, dsk, nw, nb):
    bsz, tp, _ = p.shape
    vec = lambda n: pl.BlockSpec((1, n), lambda b, c: (0, 0))
    return pl.pallas_call(
        _ssd_kernel,
        grid=(bsz // nb, tp // CHUNK),
        in_specs=[pl.BlockSpec((nb, CHUNK, P_SSD_W), lambda b, c: (b, c, 0)),
                  pl.BlockSpec((SSD_CONV, SSD_CONV_CH), lambda b, c: (0, 0)),
                  vec(SSD_CONV_CH), vec(SSD_WIDTH), vec(SSD_WIDTH), vec(SSD_WIDTH), vec(SSD_WIDTH),
                  pl.BlockSpec((CHUNK, 3 * CHUNK), lambda b, c: (0, 0))],
        out_specs=pl.BlockSpec((nb, CHUNK, SSD_WIDTH), lambda b, c: (b, c, 0)),
        out_shape=jax.ShapeDtypeStruct((bsz, tp, SSD_WIDTH), F32),
        scratch_shapes=[pltpu.VMEM((nb, CHUNK + 8, SSD_CONV_CH), F32),
                        pltpu.VMEM((nb, 2, SSD_STATE, 128), F32)],
        compiler_params=pltpu.CompilerParams(dimension_semantics=("parallel", "arbitrary")),
        name="ssd_mix",
    )(p, cw, cb, dtb, alog, dsk, nw, _tril3(CHUNK))


def _seg_sum(x, seg2_ref):
    return jnp.dot(jnp.concatenate(_split_bf16(x, 2), axis=1), seg2_ref[...], preferred_element_type=F32)


def _rwkv_kernel(p_ref, mu_ref, w0_ref, a0_ref, wa2_ref, g2_ref, kk_ref, ka_ref, rk_ref,
                 lnw_ref, lnb_ref, seg2_ref, tril3_ref, mask_ref, maskw_ref, y_ref, pbuf, state):
    c = pl.program_id(1)
    nb = p_ref.shape[0]
    C = RWKV_CHUNK
    W = RWKV_WIDTH
    R = RWKV_HEADS * C

    @pl.when(c == 0)
    def _():
        pbuf[:, 0:8, :] = jnp.zeros((nb, 8, RWKV_IN), F32)
        state[...] = jnp.zeros(state.shape, F32)

    lane = lax.broadcasted_iota(jnp.int32, (1, 128), 1)
    lane_head = lax.broadcasted_iota(jnp.int32, (1, W), 1) // RWKV_HEAD_DIM

    def stack(x):
        x = x.astype(BF16)
        zero = jnp.zeros_like(x)
        return jnp.concatenate([jnp.where(lane_head == h, x, zero) for h in range(RWKV_HEADS)], axis=0)

    def prep(b):
        pbuf[b, 8:8 + C, :] = p_ref[b]
        p = pbuf[b, 8:8 + C, :]
        p = p + (pbuf[b, 7:7 + C, :] - p) * mu_ref[...]
        pbuf[b, 0:8, :] = pbuf[b, C:C + 8, :]
        r, k, v = p[:, 0:W], p[:, W:2 * W], p[:, 2 * W:3 * W]
        lat = p[:, 3 * W:3 * W + 128]
        lat = jnp.where(lane < 64, jnp.tanh(lat), lat)
        wa = _mm(lat, wa2_ref[...])
        w = -_softplus(-(w0_ref[...] + wa[:, 0:W])) - 0.5
        logw = -jnp.exp(w)
        a = _sigmoid(a0_ref[...] + wa[:, W:2 * W])
        g = _mm(_sigmoid(p[:, 3 * W + 128:3 * W + 256]), g2_ref[...])
        kk = k * kk_ref[...]
        kk = kk / jnp.maximum(jnp.sqrt(_seg_sum(kk * kk, seg2_ref)), 1e-12)
        k2 = k * (1.0 + (a - 1.0) * ka_ref[...])
        bonus = _seg_sum(r * k2 * rk_ref[...], seg2_ref) * v
        cum = _cumsum_rows(logw, tril3_ref)
        e_neg = jnp.exp(-cum)
        pc = jnp.exp(cum[C - 1:C, :])
        bt = kk * a * e_neg
        kt = k2 * e_neg
        return dict(
            lhs=jnp.concatenate([-kk * jnp.exp(cum - logw), r * jnp.exp(cum)], axis=0).astype(BF16),
            rhs=jnp.concatenate([stack(bt), stack(kt)], axis=0),
            upd=jnp.concatenate([bt * pc, kt * pc], axis=0).astype(BF16),
            v=v.astype(BF16), v_bd=stack(v), pc=pc, bonus=bonus, g=g)

    def scores(d):
        sc = lax.dot_general(d["lhs"], d["rhs"], (((1,), (1,)), ((), ())), preferred_element_type=F32)
        strict_w, incl_w = maskw_ref[0], maskw_ref[1]
        d["npow"] = sc[0:C, 0:R] * strict_w
        d["n_bd"] = stack(d["npow"])
        d["a_k"] = jnp.concatenate([sc[0:C, R:2 * R] * strict_w, sc[C:2 * C, R:2 * R] * incl_w],
                                   axis=0).astype(BF16)
        d["a_rb"] = (sc[C:2 * C, 0:R] * incl_w).astype(BF16)
        d["tinv"] = maskw_ref[2] + d["npow"]

    def double(d):
        d["npow"] = jnp.dot(d["npow"].astype(BF16), d["n_bd"], preferred_element_type=F32)
        d["n_bd"] = stack(d["npow"])
        d["tinv"] = d["tinv"] + jnp.dot(d["tinv"].astype(BF16), d["n_bd"], preferred_element_type=F32)

    def read_state(b, d):
        d["sxav"] = _mm_nt(d["lhs"], state[b]) + jnp.dot(d["a_k"], d["v_bd"], preferred_element_type=F32)

    def correction(d):
        d["u"] = jnp.dot(d["tinv"].astype(BF16), stack(d["sxav"][0:C]), preferred_element_type=F32)

    def write_state(b, d):
        u = d["u"]
        d["y"] = d["sxav"][C:2 * C] + jnp.dot(d["a_rb"], stack(u), preferred_element_type=F32)
        new = _mm_tn(jnp.concatenate([u.astype(BF16), d["v"]], axis=0), d["upd"])
        state[b] = state[b] * d["pc"] + new * mask_ref[...]

    def finish(b, d):
        inv_n = 1.0 / RWKV_HEAD_DIM
        yc = d["y"] - _seg_sum(d["y"], seg2_ref) * inv_n
        var = _seg_sum(yc * yc, seg2_ref) * inv_n
        yn = yc * lax.rsqrt(var + RWKV_GN_EPS) * lnw_ref[...] + lnb_ref[...]
        y_ref[b] = (yn + d["bonus"]) * d["g"]

    ds = [prep(b) for b in range(nb)]
    for d in ds:
        scores(d)
    n = 1
    while 2 * n < C:
        for d in ds:
            double(d)
        n *= 2
    for b, d in enumerate(ds):
        read_state(b, d)
    for d in ds:
        correction(d)
    for b, d in enumerate(ds):
        write_state(b, d)
    for b, d in enumerate(ds):
        finish(b, d)


def _rwkv_tables():
    C, W, R = RWKV_CHUNK, RWKV_WIDTH, RWKV_HEADS * RWKV_CHUNK
    head = np.arange(W) // RWKV_HEAD_DIM
    seg = (head[:, None] == head[None, :]).astype(np.float32)
    assert C == RWKV_HEAD_DIM
    cc = np.arange(R)[None, :]
    tt = np.arange(C)[:, None]
    wide = np.stack([(cc % C) < tt, (cc % C) <= tt, (cc % C) == tt])
    return (jnp.asarray(np.concatenate([seg, seg], axis=0), BF16), _tril3(C), jnp.asarray(seg, F32),
            jnp.asarray(wide.astype(np.float32), F32))


def _rwkv_mix(p, mu, w0, a0, wa2, g2, kk, ka, rk, lnw, lnb, nb):
    bsz, tp, _ = p.shape
    C, W, R = RWKV_CHUNK, RWKV_WIDTH, RWKV_HEADS * RWKV_CHUNK
    vec = lambda n: pl.BlockSpec((1, n), lambda b, c: (0, 0))
    full = lambda s: pl.BlockSpec(s, lambda b, c: (0,) * len(s))
    return pl.pallas_call(
        _rwkv_kernel,
        grid=(bsz // nb, tp // C),
        in_specs=[pl.BlockSpec((nb, C, RWKV_IN), lambda b, c: (b, c, 0)),
                  vec(RWKV_IN), vec(W), vec(W), full((128, 2 * W)), full((128, W)),
                  vec(W), vec(W), vec(W), vec(W), vec(W),
                  full((2 * W, W)), full((C, 3 * C)), full((W, W)), full((3, C, R))],
        out_specs=pl.BlockSpec((nb, C, W), lambda b, c: (b, c, 0)),
        out_shape=jax.ShapeDtypeStruct((bsz, tp, W), F32),
        scratch_shapes=[pltpu.VMEM((nb, C + 8, RWKV_IN), F32),
                        pltpu.VMEM((nb, W, W), F32)],
        compiler_params=pltpu.CompilerParams(dimension_semantics=("parallel", "arbitrary")),
        name="rwkv_mix",
    )(p, mu, w0, a0, wa2, g2, kk, ka, rk, lnw, lnb, *_rwkv_tables())


def _lru_kernel(p_ref, cw_ref, cb_ref, wax_ref, bax_ref, lam_ref, y_ref, xbuf, hprev):
    c = pl.program_id(1)
    nb = p_ref.shape[0]
    L = CHUNK
    W = LRU_WIDTH

    @pl.when(c == 0)
    def _():
        xbuf[:, 0:8, :] = jnp.zeros((nb, 8, W), F32)
        hprev[...] = jnp.zeros(hprev.shape, F32)

    rowi = lax.broadcasted_iota(jnp.int32, (L, 1), 0)
    log_a_unit = -LRU_C * _softplus(-lam_ref[...])
    for b in range(nb):
        buf = xbuf.at[b]
        buf[8:8 + L, :] = p_ref[b, :, 0:W]
        xc = _causal_conv(buf, cw_ref[...], cb_ref[...], L)
        buf[0:8, :] = buf[L:L + 8, :]
        gates = _sigmoid(_mm(xc, wax_ref[...]) + bax_ref[...])
        r, i = gates[:, 0:W], gates[:, W:2 * W]
        log_a = r * log_a_unit
        a = jnp.exp(log_a)
        t = jnp.tanh(log_a)
        one_minus_a2 = -2.0 * t / (1.0 - t)
        u = jnp.sqrt(one_minus_a2) * (i * xc)
        u = jnp.where(c * L + rowi >= PAD, u, 0.0)

        s = 1
        while s < 8:
            keep = (rowi & 7) >= s
            a_sh = jnp.where(keep, pltpu.roll(a, s, axis=0), 1.0)
            u_sh = jnp.where(keep, pltpu.roll(u, s, axis=0), 0.0)
            u = u + a * u_sh
            a = a * a_sh
            s *= 2
        carry = hprev[b, 0:1, :]
        groups = []
        for i in range(L // 8):
            h_i = u[8 * i:8 * i + 8, :] + a[8 * i:8 * i + 8, :] * carry
            groups.append(h_i)
            carry = h_i[7:8, :]
        hprev[b, 0:1, :] = carry
        h = jnp.concatenate(groups, axis=0)
        y_ref[b] = h * jax.nn.gelu(p_ref[b, :, W:2 * W], approximate=True)


def _lru_mix(p, cw, cb, wax, bax, lam, nb):
    bsz, tp, _ = p.shape
    vec = lambda n: pl.BlockSpec((1, n), lambda b, c: (0, 0))
    return pl.pallas_call(
        _lru_kernel,
        grid=(bsz // nb, tp // CHUNK),
        in_specs=[pl.BlockSpec((nb, CHUNK, LRU_IN), lambda b, c: (b, c, 0)),
                  pl.BlockSpec((LRU_CONV, LRU_WIDTH), lambda b, c: (0, 0)),
                  vec(LRU_WIDTH),
                  pl.BlockSpec((LRU_WIDTH, 2 * LRU_WIDTH), lambda b, c: (0, 0)),
                  vec(2 * LRU_WIDTH), vec(LRU_WIDTH)],
        out_specs=pl.BlockSpec((nb, CHUNK, LRU_WIDTH), lambda b, c: (b, c, 0)),
        out_shape=jax.ShapeDtypeStruct((bsz, tp, LRU_WIDTH), F32),
        scratch_shapes=[pltpu.VMEM((nb, CHUNK + 8, LRU_WIDTH), F32),
                        pltpu.VMEM((nb, 8, LRU_WIDTH), F32)],
        compiler_params=pltpu.CompilerParams(dimension_semantics=("parallel", "arbitrary")),
        name="lru_mix",
    )(p, cw, cb, wax, bax, lam)


def _ret_kernel(p_ref, cos_ref, sin_ref, dmat_ref, kdec_ref, qdec_ref, cross_ref, bd_ref, seg2_ref,
                gnw_ref, y_ref, state):
    c = pl.program_id(1)
    nb = p_ref.shape[0]
    W = RET_WIDTH

    @pl.when(c == 0)
    def _():
        state[...] = jnp.zeros(state.shape, F32)

    lane = lax.broadcasted_iota(jnp.int32, (1, RET_QK), 1)
    first_half = (lane % RET_QK_DIM) < (RET_QK_DIM // 2)
    half = RET_QK_DIM // 2
    qk_head = lane // RET_QK_DIM
    v_head = lax.broadcasted_iota(jnp.int32, (1, W), 1) // RET_V_DIM

    def rope(x):
        swapped = jnp.where(first_half, pltpu.roll(x, RET_QK - half, axis=1), pltpu.roll(x, half, axis=1))
        return x * cos_ref[...] + swapped * sin_ref[...]

    def prep(b):
        q = rope(p_ref[b, :, 0:RET_QK])
        k = rope(p_ref[b, :, RET_QK:2 * RET_QK]) * (RET_QK_DIM ** -0.5)
        v = p_ref[b, :, 2 * RET_QK:2 * RET_QK + W]
        return dict(q=q, k32=k, k=k.astype(BF16), v=v.astype(BF16), v32=v)

    def scores(d):
        d["s"] = [(_mm_nt(jnp.where(qk_head == h, d["q"], 0.0), d["k"]) * dmat_ref[h]).astype(BF16)
                  for h in range(RET_HEADS)]

    def outputs(b, d):
        y = _mm(d["q"] * qdec_ref[...], state[b])
        for h in range(RET_HEADS):
            y = y + jnp.dot(d["s"][h], jnp.where(v_head == h, d["v32"], 0.0).astype(BF16),
                            preferred_element_type=F32)
        d["y"] = y
        kv = _mm_tn(d["k32"] * kdec_ref[...], d["v"]) * bd_ref[...]
        state[b] = state[b] * cross_ref[...] + kv

    def center(d):
        d["yc"] = d["y"] - _seg_sum(d["y"], seg2_ref) * (1.0 / RET_V_DIM)

    def finish(b, d):
        var = _seg_sum(d["yc"] * d["yc"], seg2_ref) * (1.0 / RET_V_DIM)
        g = p_ref[b, :, 2 * RET_QK + W:2 * RET_QK + 2 * W]
        y_ref[b] = d["yc"] * lax.rsqrt(var + RET_GN_EPS) * gnw_ref[...] * _silu(g)

    ds = [prep(b) for b in range(nb)]
    for d in ds:
        scores(d)
    for b, d in enumerate(ds):
        outputs(b, d)
    for d in ds:
        center(d)
    for b, d in enumerate(ds):
        finish(b, d)


def _ret_mix(p, cos_t, sin_t, dmat, kdec, qdec, cross, bd, seg2, gnw, nb):
    bsz, tp, _ = p.shape
    const2 = lambda s: pl.BlockSpec(s, lambda b, c: (0, 0))
    return pl.pallas_call(
        _ret_kernel,
        grid=(bsz // nb, tp // CHUNK),
        in_specs=[pl.BlockSpec((nb, CHUNK, RET_IN), lambda b, c: (b, c, 0)),
                  pl.BlockSpec((CHUNK, RET_QK), lambda b, c: (c, 0)),
                  pl.BlockSpec((CHUNK, RET_QK), lambda b, c: (c, 0)),
                  pl.BlockSpec((RET_HEADS, CHUNK, CHUNK), lambda b, c: (0, 0, 0)),
                  const2((CHUNK, RET_QK)), const2((CHUNK, RET_QK)),
                  const2((RET_QK, RET_WIDTH)), const2((RET_QK, RET_WIDTH)),
                  const2((2 * RET_WIDTH, RET_WIDTH)), const2((1, RET_WIDTH))],
        out_specs=pl.BlockSpec((nb, CHUNK, RET_WIDTH), lambda b, c: (b, c, 0)),
        out_shape=jax.ShapeDtypeStruct((bsz, tp, RET_WIDTH), F32),
        scratch_shapes=[pltpu.VMEM((nb, RET_QK, RET_WIDTH), F32)],
        compiler_params=pltpu.CompilerParams(dimension_semantics=("parallel", "arbitrary")),
        name="ret_mix",
    )(p, cos_t, sin_t, dmat, kdec, qdec, cross, bd, seg2, gnw)


def _retention_tables(tp):
    half = RET_QK_DIM // 2
    lane = np.arange(RET_QK)
    freqs = ROPE_BASE ** (-np.arange(half, dtype=np.float64) / half)
    pos = np.arange(tp, dtype=np.float64) - PAD
    ang = pos[:, None] * freqs[lane % half][None, :]
    sign = np.where((lane % RET_QK_DIM) < half, -1.0, 1.0)
    cos_t = np.cos(ang)
    sin_t = np.sin(ang) * sign[None, :]
    log_g = np.log1p(-np.exp2(-5.0 - np.arange(RET_HEADS, dtype=np.float64)))
    idx = np.arange(CHUNK)
    rel = idx[:, None] - idx[None, :]
    dmat = np.where(rel >= 0, np.exp(np.maximum(rel, 0)[None] * log_g[:, None, None]), 0.0)
    lg_lane = log_g[lane // RET_QK_DIM]
    kdec = np.exp((CHUNK - 1 - idx)[:, None] * lg_lane[None, :])
    qdec = np.exp((idx + 1)[:, None] * lg_lane[None, :])
    cross = np.broadcast_to(np.exp(CHUNK * lg_lane)[:, None], (RET_QK, RET_WIDTH))
    v_head = np.arange(RET_WIDTH) // RET_V_DIM
    bd = (lane // RET_QK_DIM)[:, None] == v_head[None, :]
    seg = (v_head[:, None] == v_head[None, :]).astype(np.float32)
    f = lambda x: jnp.asarray(np.ascontiguousarray(x), F32)
    return (f(cos_t), f(sin_t), f(dmat), f(kdec), f(qdec), f(cross), f(bd),
            jnp.asarray(np.concatenate([seg, seg], axis=0), BF16))


def _out_ffn_kernel(h_ref, y0_ref, y1_ref, y2_ref, y3_ref, wo_ref, n1_ref, n2_ref, n3_ref,
                    wg_ref, wu_ref, wd_ref, o_ref, *, tf):
    u = None
    for i, yr in enumerate((y0_ref, y1_ref, y2_ref, y3_ref)):
        part = _mm(yr[...], wo_ref[256 * i:256 * i + 256, :])
        u = part if u is None else u + part
    h1 = h_ref[...] + _rms(u, n1_ref[...])
    hn = _rms(h1, n2_ref[...]).astype(BF16)
    acc = None
    for j in range(D_FF // tf):
        sl = slice(j * tf, (j + 1) * tf)
        act = _silu(jnp.dot(hn, wg_ref[:, sl], preferred_element_type=F32)) * \
            jnp.dot(hn, wu_ref[:, sl], preferred_element_type=F32)
        part = jnp.dot(act.astype(BF16), wd_ref[sl, :], preferred_element_type=F32)
        acc = part if acc is None else acc + part
    o_ref[...] = h1 + _rms(acc, n3_ref[...])


def _out_ffn(h, ys, wo, n1, n2, n3, wg, wu, wd, layer, tm, tf):
    m, d = h.shape
    row = lambda n: pl.BlockSpec((tm, n), lambda i: (i, 0))
    full = lambda s: pl.BlockSpec(s, lambda i: (0, 0))
    stacked = lambda s: pl.BlockSpec((None,) + s, lambda i: (layer, 0, 0))
    return pl.pallas_call(
        functools.partial(_out_ffn_kernel, tf=tf),
        grid=(m // tm,),
        in_specs=[row(d), row(256), row(256), row(256), row(256),
                  stacked((d, d)), full((1, d)), full((1, d)), full((1, d)),
                  stacked((d, D_FF)), stacked((d, D_FF)), stacked((D_FF, d))],
        out_specs=row(d),
        out_shape=jax.ShapeDtypeStruct((m, d), F32),
        compiler_params=pltpu.CompilerParams(
            dimension_semantics=("parallel",), vmem_limit_bytes=VMEM_LIMIT_BYTES),
        name="out_ffn",
    )(h, *ys, wo, n1, n2, n3, wg, wu, wd)


def _block_diag(blocks):
    g, n, m = blocks.shape
    eye = jnp.eye(g, dtype=blocks.dtype)
    return (eye[:, None, :, None] * blocks[:, :, None, :]).reshape(g * n, g * m)


def _row_tile(tp):
    best = 8
    for t in range(8, 641, 8):
        if tp % t == 0:
            best = t
    return best


def kernel(x, meta_tokens, pre_mix_norm, post_mix_norm, pre_ffn_norm, post_ffn_norm, w_in, w_out, ssd_conv_w, ssd_conv_b, ssd_dt_bias, ssd_a_log, ssd_d, ssd_norm_w, rwkv_mu, rwkv_w0, rwkv_w2, rwkv_a0, rwkv_a2, rwkv_g2, rwkv_k_k, rwkv_k_a, rwkv_r_k, rwkv_ln_w, rwkv_ln_b, lru_conv_w, lru_conv_b, lru_wa, lru_ba, lru_wx, lru_bx, lru_lambda, ret_gn_w, ffn_w_gate, ffn_w_up, ffn_w_down):
    bsz, seq, d = x.shape
    depth = w_in.shape[0]
    t = N_META + seq
    tp = t + PAD
    assert d == D_MODEL and tp % CHUNK == 0
    meta = jnp.broadcast_to(meta_tokens.astype(x.dtype)[None], (bsz, N_META, d))
    h = jnp.concatenate([jnp.zeros((bsz, PAD, d), x.dtype), meta, x], axis=1)

    tm_in = _row_tile(tp)
    m_rows = bsz * tp
    tm_ffn = 512 if m_rows % 512 == 0 else CHUNK
    tf = 256
    tables = _retention_tables(tp)
    per_head = lambda v: jnp.repeat(v, SSD_HEAD_DIM, axis=-1)[None]
    r2 = lambda v: v[None]

    w_cat = _prep_w_in(w_in)
    nb = next(n for n in (8, 4, 2, 1) if bsz % n == 0)
    nb_rwkv = nb
    wo16, wg16, wu16, wd16 = (w.astype(BF16) for w in (w_out, ffn_w_gate, ffn_w_up, ffn_w_down))

    for l in range(depth):
        p_ssd, p_rwkv, p_lru, p_ret = _in_proj(h, r2(pre_mix_norm[l]), w_cat, l, tm_in)

        y_ssd = _ssd_mix(p_ssd, ssd_conv_w[l], r2(ssd_conv_b[l]), per_head(ssd_dt_bias[l]),
                         per_head(ssd_a_log[l]), per_head(ssd_d[l]), r2(ssd_norm_w[l]), nb)

        zero = jnp.zeros((64, RWKV_WIDTH), F32)
        wa2 = jnp.concatenate([jnp.concatenate([rwkv_w2[l], zero], axis=1),
                               jnp.concatenate([zero, rwkv_a2[l]], axis=1)], axis=0).astype(BF16)
        y_rwkv = _rwkv_mix(p_rwkv, r2(rwkv_mu[l]), r2(rwkv_w0[l]), r2(rwkv_a0[l]), wa2,
                           rwkv_g2[l].astype(BF16), r2(rwkv_k_k[l]), r2(rwkv_k_a[l]),
                           rwkv_r_k[l].reshape(1, RWKV_WIDTH), r2(rwkv_ln_w[l]), r2(rwkv_ln_b[l]),
                           nb_rwkv)

        wax = jnp.concatenate([_block_diag(lru_wa[l]), _block_diag(lru_wx[l])], axis=1).astype(BF16)
        bax = jnp.concatenate([lru_ba[l], lru_bx[l]])[None]
        y_lru = _lru_mix(p_lru, lru_conv_w[l], r2(lru_conv_b[l]), wax, bax, r2(lru_lambda[l]), nb)

        y_ret = _ret_mix(p_ret, *tables, r2(ret_gn_w[l]), nb)

        flat = lambda a: a.reshape(m_rows, a.shape[-1])
        h = _out_ffn(flat(h), [flat(y_ssd), flat(y_rwkv), flat(y_lru), flat(y_ret)],
                     wo16, r2(post_mix_norm[l]), r2(pre_ffn_norm[l]), r2(post_ffn_norm[l]),
                     wg16, wu16, wd16, l, tm_ffn, tf).reshape(bsz, tp, d)
    return h[:, PAD + N_META:]
```

```python
import functools
import math

import numpy as np
import jax
import jax.numpy as jnp
from jax import lax
from jax.experimental import pallas as pl
from jax.experimental.pallas import tpu as pltpu

F32 = jnp.float32
BF16 = jnp.bfloat16

D_MODEL = 1024
N_META = 16
CHUNK = 128
PAD = CHUNK - N_META
NORM_EPS = 1e-6

SSD_HEADS = 4
SSD_HEAD_DIM = 64
SSD_WIDTH = 256
SSD_STATE = 128
SSD_CONV = 4
SSD_CONV_CH = 768
SSD_IN = 1028

RWKV_HEADS = 4
RWKV_HEAD_DIM = 64
RWKV_WIDTH = 256
RWKV_IN = 1024
RWKV_GN_EPS = 64e-5
RWKV_CHUNK = 64

LRU_WIDTH = 256
LRU_CONV = 4
LRU_C = 8.0
LRU_IN = 512

RET_HEADS = 4
RET_QK_DIM = 32
RET_V_DIM = 64
RET_WIDTH = 256
RET_QK = RET_HEADS * RET_QK_DIM
RET_IN = 768
RET_GN_EPS = 1e-5
ROPE_BASE = 10000.0

D_FF = 2816

VMEM_LIMIT_BYTES = 52 * 1024 * 1024

P_SSD_W = SSD_WIDTH + SSD_CONV_CH + SSD_WIDTH
IN_COLS = P_SSD_W + RWKV_IN + LRU_IN + RET_IN


def _mm(a, b):
    return jnp.dot(a.astype(BF16), b.astype(BF16), preferred_element_type=F32)


def _mm_nt(a, b):
    return lax.dot_general(a.astype(BF16), b.astype(BF16), (((1,), (1,)), ((), ())),
                           preferred_element_type=F32)


def _mm_tn(a, b):
    return lax.dot_general(a.astype(BF16), b.astype(BF16), (((0,), (0,)), ((), ())),
                           preferred_element_type=F32)


def _mm_exact(a, b):
    return jnp.dot(a, b, preferred_element_type=F32, precision=lax.Precision.HIGHEST)


def _sigmoid(x):
    return 1.0 / (1.0 + jnp.exp(-x))


def _silu(x):
    return x * _sigmoid(x)


def _softplus(x):
    return jnp.maximum(x, 0.0) + jnp.log1p(jnp.exp(-jnp.abs(x)))


def _rms(x, w):
    return x * lax.rsqrt(jnp.mean(x * x, axis=-1, keepdims=True) + NORM_EPS) * w


def _causal_conv(buf, w, b, rows):
    acc = b + w[3:4, :] * buf[8:8 + rows, :]
    acc = acc + w[2:3, :] * buf[7:7 + rows, :]
    acc = acc + w[1:2, :] * buf[6:6 + rows, :]
    acc = acc + w[0:1, :] * buf[5:5 + rows, :]
    return acc


def _prep_w_in_kernel(w_ref, dt_ref, o_ref):
    main = SSD_WIDTH + SSD_CONV_CH
    o_ref[:, 0:main] = w_ref[:, 0:main]
    o_ref[:, main:P_SSD_W] = dt_ref[...]
    o_ref[:, P_SSD_W:] = w_ref[:, SSD_IN:]


def _prep_w_in(w_in):
    depth, d, n = w_in.shape
    tr = 256
    main = SSD_WIDTH + SSD_CONV_CH
    w16 = w_in.astype(BF16)
    dt_rep = jnp.repeat(w_in[:, :, main:SSD_IN], SSD_HEAD_DIM, axis=2).astype(BF16)
    return pl.pallas_call(
        _prep_w_in_kernel,
        grid=(depth, d // tr),
        in_specs=[pl.BlockSpec((None, tr, n), lambda l, i: (l, i, 0)),
                  pl.BlockSpec((None, tr, SSD_WIDTH), lambda l, i: (l, i, 0))],
        out_specs=pl.BlockSpec((None, tr, IN_COLS), lambda l, i: (l, i, 0)),
        out_shape=jax.ShapeDtypeStruct((depth, d, IN_COLS), BF16),
        compiler_params=pltpu.CompilerParams(dimension_semantics=("parallel", "parallel")),
        name="prep_w_in",
    )(w16, dt_rep)


def _in_proj_kernel(h_ref, nw_ref, w_ref, ssd_ref, rwkv_ref, lru_ref, ret_ref):
    tm = h_ref.shape[0]
    row = pl.program_id(1) * tm + lax.broadcasted_iota(jnp.int32, (tm, 1), 0)
    hn = _rms(h_ref[...], nw_ref[...])
    hn = jnp.where(row >= PAD, hn, 0.0).astype(BF16)
    o = 0
    for ref in (ssd_ref, rwkv_ref, lru_ref, ret_ref):
        n = ref.shape[1]
        ref[...] = jnp.dot(hn, w_ref[:, o:o + n], preferred_element_type=F32)
        o += n


def _in_proj(h, nw, w, layer, tm):
    bsz, tp, d = h.shape
    widths = (P_SSD_W, RWKV_IN, LRU_IN, RET_IN)
    return pl.pallas_call(
        _in_proj_kernel,
        grid=(bsz, tp // tm),
        in_specs=[pl.BlockSpec((None, tm, d), lambda b, i: (b, i, 0)),
                  pl.BlockSpec((1, d), lambda b, i: (0, 0)),
                  pl.BlockSpec((None, d, IN_COLS), lambda b, i: (layer, 0, 0))],
        out_specs=[pl.BlockSpec((None, tm, n), lambda b, i: (b, i, 0)) for n in widths],
        out_shape=[jax.ShapeDtypeStruct((bsz, tp, n), F32) for n in widths],
        compiler_params=pltpu.CompilerParams(
            dimension_semantics=("parallel", "arbitrary"), vmem_limit_bytes=VMEM_LIMIT_BYTES),
        name="in_proj",
    )(h, nw, w)


def _split_bf16(x, parts):
    out = []
    for _ in range(parts):
        hi = x.astype(BF16)
        out.append(hi)
        x = x - hi.astype(F32)
    return out


def _cumsum_rows(x, tril3_ref):
    return jnp.dot(tril3_ref[...], jnp.concatenate(_split_bf16(x, 3), axis=0), preferred_element_type=F32)


def _tril3(n):
    tril = (np.arange(n)[:, None] >= np.arange(n)[None, :]).astype(np.float32)
    return jnp.asarray(np.concatenate([tril] * 3, axis=1), BF16)


def _ssd_kernel(p_ref, cw_ref, cb_ref, dtb_ref, alog_ref, dsk_ref, nw_ref, tril3_ref, y_ref, xbuf, state):
    c = pl.program_id(1)
    nb = p_ref.shape[0]
    L = CHUNK

    @pl.when(c == 0)
    def _():
        xbuf[:, 0:8, :] = jnp.zeros((nb, 8, SSD_CONV_CH), F32)
        state[...] = jnp.zeros(state.shape, F32)

    row = c * L + lax.broadcasted_iota(jnp.int32, (L, 1), 0)
    causal = lax.broadcasted_iota(jnp.int32, (L, L), 0) >= lax.broadcasted_iota(jnp.int32, (L, L), 1)
    lane_lo = lax.broadcasted_iota(jnp.int32, (1, 128), 1) < SSD_HEAD_DIM
    groups = [slice(128 * g, 128 * g + 128) for g in range(2)]

    def prep(b):
        buf = xbuf.at[b]
        buf[8:8 + L, :] = p_ref[b, :, SSD_WIDTH:SSD_WIDTH + SSD_CONV_CH]
        xbc = _silu(_causal_conv(buf, cw_ref[...], cb_ref[...], L))
        buf[0:8, :] = buf[L:L + 8, :]
        dt = _softplus(p_ref[b, :, SSD_WIDTH + SSD_CONV_CH:P_SSD_W] + dtb_ref[...])
        dt = jnp.where(row >= PAD, dt, 0.0)
        a = dt * (-jnp.exp(alog_ref[...]))
        xs = xbc[:, 0:256]
        return dict(xs=xs, bs=xbc[:, 256:512], cs=xbc[:, 512:768], xdt=xs * dt,
                    acs=_cumsum_rows(a, tril3_ref), y=[None, None])

    def gram(d, g):
        sl = groups[g]
        return _mm_nt(d["cs"][:, sl], d["bs"][:, sl])

    def chunk(b, d, g, gmat):
        sl = groups[g]
        acs_g = d["acs"][:, sl]
        acs_t = acs_g.T
        a_last = acs_g[L - 1:L, :]
        b_g, c_g, x_g = d["bs"][:, sl], d["cs"][:, sl], d["xdt"][:, sl]
        ms = []
        for hh in range(2):
            o = SSD_HEAD_DIM * hh
            diff = acs_g[:, o:o + 1] - acs_t[o:o + 1, :]
            ms.append(gmat * jnp.exp(jnp.where(causal, diff, -1e30)))
        mcat = jnp.concatenate(ms, axis=1)
        xbd = jnp.concatenate([jnp.where(lane_lo, x_g, 0.0), jnp.where(lane_lo, 0.0, x_g)], axis=0)
        s_in = state[b, g]
        d["y"][g] = _mm(mcat, xbd) + _mm(c_g, s_in) * jnp.exp(acs_g)
        state[b, g] = s_in * jnp.exp(a_last) + _mm_tn(b_g, x_g * jnp.exp(a_last - acs_g))

    def finish(b, d, g):
        sl = groups[g]
        y_g = (d["y"][g] + dsk_ref[:, sl] * d["xs"][:, sl]) * _silu(p_ref[b, :, sl])
        y_ref[b, :, sl] = _rms(y_g, nw_ref[:, sl])

    ds = [prep(b) for b in range(nb)]
    gm = [[gram(d, g) for g in range(2)] for d in ds]
    for b, d in enumerate(ds):
        for g in range(2):
            chunk(b, d, g, gm[b][g])
    for b, d in enumerate(ds):
        for g in range(2):
            finish(b, d, g)


def _ssd_mix(p, cw, cb, dtb, alog, dsk, nw, nb):
    bsz, tp, _ = p.shape
    vec = lambda n: pl.BlockSpec((1, n), lambda b, c: (0, 0))
    return pl.pallas_call(
        _ssd_kernel,
        grid=(bsz // nb, tp // CHUNK),
        in_specs=[pl.BlockSpec((nb, CHUNK, P_SSD_W), lambda b, c: (b, c, 0)),
                  pl.BlockSpec((SSD_CONV, SSD_CONV_CH), lambda b, c: (0, 0)),
                  vec(SSD_CONV_CH), vec(SSD_WIDTH), vec(SSD_WIDTH), vec(SSD_WIDTH), vec(SSD_WIDTH),
                  pl.BlockSpec((CHUNK, 3 * CHUNK), lambda b, c: (0, 0))],
        out_specs=pl.BlockSpec((nb, CHUNK, SSD_WIDTH), lambda b, c: (b, c, 0)),
        out_shape=jax.ShapeDtypeStruct((bsz, tp, SSD_WIDTH), F32),
        scratch_shapes=[pltpu.VMEM((nb, CHUNK + 8, SSD_CONV_CH), F32),
                        pltpu.VMEM((nb, 2, SSD_STATE, 128), F32)],
        compiler_params=pltpu.CompilerParams(dimension_semantics=("parallel", "arbitrary")),
        name="ssd_mix",
    )(p, cw, cb, dtb, alog, dsk, nw, _tril3(CHUNK))


def _seg_sum(x, seg2_ref):
    return jnp.dot(jnp.concatenate(_split_bf16(x, 2), axis=1), seg2_ref[...], preferred_element_type=F32)


def _rwkv_kernel(p_ref, mu_ref, w0_ref, a0_ref, wa2_ref, g2_ref, kk_ref, ka_ref, rk_ref,
                 lnw_ref, lnb_ref, seg2_ref, tril3_ref, mask_ref, maskw_ref, y_ref, pbuf, state):
    c = pl.program_id(1)
    nb = p_ref.shape[0]
    C = RWKV_CHUNK
    W = RWKV_WIDTH
    R = RWKV_HEADS * C

    @pl.when(c == 0)
    def _():
        pbuf[:, 0:8, :] = jnp.zeros((nb, 8, RWKV_IN), F32)
        state[...] = jnp.zeros(state.shape, F32)

    lane = lax.broadcasted_iota(jnp.int32, (1, 128), 1)
    lane_head = lax.broadcasted_iota(jnp.int32, (1, W), 1) // RWKV_HEAD_DIM

    def stack(x):
        x = x.astype(BF16)
        zero = jnp.zeros_like(x)
        return jnp.concatenate([jnp.where(lane_head == h, x, zero) for h in range(RWKV_HEADS)], axis=0)

    def prep(b):
        pbuf[b, 8:8 + C, :] = p_ref[b]
        p = pbuf[b, 8:8 + C, :]
        p = p + (pbuf[b, 7:7 + C, :] - p) * mu_ref[...]
        pbuf[b, 0:8, :] = pbuf[b, C:C + 8, :]
        r, k, v = p[:, 0:W], p[:, W:2 * W], p[:, 2 * W:3 * W]
        lat = p[:, 3 * W:3 * W + 128]
        lat = jnp.where(lane < 64, jnp.tanh(lat), lat)
        wa = _mm(lat, wa2_ref[...])
        w = -_softplus(-(w0_ref[...] + wa[:, 0:W])) - 0.5
        logw = -jnp.exp(w)
        a = _sigmoid(a0_ref[...] + wa[:, W:2 * W])
        g = _mm(_sigmoid(p[:, 3 * W + 128:3 * W + 256]), g2_ref[...])
        kk = k * kk_ref[...]
        kk = kk / jnp.maximum(jnp.sqrt(_seg_sum(kk * kk, seg2_ref)), 1e-12)
        k2 = k * (1.0 + (a - 1.0) * ka_ref[...])
        bonus = _seg_sum(r * k2 * rk_ref[...], seg2_ref) * v
        cum = _cumsum_rows(logw, tril3_ref)
        e_neg = jnp.exp(-cum)
        pc = jnp.exp(cum[C - 1:C, :])
        bt = kk * a * e_neg
        kt = k2 * e_neg
        return dict(
            lhs=jnp.concatenate([-kk * jnp.exp(cum - logw), r * jnp.exp(cum)], axis=0).astype(BF16),
            rhs=jnp.concatenate([stack(bt), stack(kt)], axis=0),
            upd=jnp.concatenate([bt * pc, kt * pc], axis=0).astype(BF16),
            v=v.astype(BF16), v_bd=stack(v), pc=pc, bonus=bonus, g=g)

    def scores(d):
        sc = lax.dot_general(d["lhs"], d["rhs"], (((1,), (1,)), ((), ())), preferred_element_type=F32)
        strict_w, incl_w = maskw_ref[0], maskw_ref[1]
        d["npow"] = sc[0:C, 0:R] * strict_w
        d["n_bd"] = stack(d["npow"])
        d["a_k"] = jnp.concatenate([sc[0:C, R:2 * R] * strict_w, sc[C:2 * C, R:2 * R] * incl_w],
                                   axis=0).astype(BF16)
        d["a_rb"] = (sc[C:2 * C, 0:R] * incl_w).astype(BF16)
        d["tinv"] = maskw_ref[2] + d["npow"]

    def double(d):
        d["npow"] = jnp.dot(d["npow"].astype(BF16), d["n_bd"], preferred_element_type=F32)
        d["n_bd"] = stack(d["npow"])
        d["tinv"] = d["tinv"] + jnp.dot(d["tinv"].astype(BF16), d["n_bd"], preferred_element_type=F32)

    def read_state(b, d):
        d["sxav"] = _mm_nt(d["lhs"], state[b]) + jnp.dot(d["a_k"], d["v_bd"], preferred_element_type=F32)

    def correction(d):
        d["u"] = jnp.dot(d["tinv"].astype(BF16), stack(d["sxav"][0:C]), preferred_element_type=F32)

    def write_state(b, d):
        u = d["u"]
        d["y"] = d["sxav"][C:2 * C] + jnp.dot(d["a_rb"], stack(u), preferred_element_type=F32)
        new = _mm_tn(jnp.concatenate([u.astype(BF16), d["v"]], axis=0), d["upd"])
        state[b] = state[b] * d["pc"] + new * mask_ref[...]

    def finish(b, d):
        inv_n = 1.0 / RWKV_HEAD_DIM
        yc = d["y"] - _seg_sum(d["y"], seg2_ref) * inv_n
        var = _seg_sum(yc * yc, seg2_ref) * inv_n
        yn = yc * lax.rsqrt(var + RWKV_GN_EPS) * lnw_ref[...] + lnb_ref[...]
        y_ref[b] = (yn + d["bonus"]) * d["g"]

    ds = [prep(b) for b in range(nb)]
    for d in ds:
        scores(d)
    n = 1
    while 2 * n < C:
        for d in ds:
            double(d)
        n *= 2
    for b, d in enumerate(ds):
        read_state(b, d)
    for d in ds:
        correction(d)
    for b, d in enumerate(ds):
        write_state(b, d)
    for b, d in enumerate(ds):
        finish(b, d)


def _rwkv_tables():
    C, W, R = RWKV_CHUNK, RWKV_WIDTH, RWKV_HEADS * RWKV_CHUNK
    head = np.arange(W) // RWKV_HEAD_DIM
    seg = (head[:, None] == head[None, :]).astype(np.float32)
    assert C == RWKV_HEAD_DIM
    cc = np.arange(R)[None, :]
    tt = np.arange(C)[:, None]
    wide = np.stack([(cc % C) < tt, (cc % C) <= tt, (cc % C) == tt])
    return (jnp.asarray(np.concatenate([seg, seg], axis=0), BF16), _tril3(C), jnp.asarray(seg, F32),
            jnp.asarray(wide.astype(np.float32), F32))


def _rwkv_mix(p, mu, w0, a0, wa2, g2, kk, ka, rk, lnw, lnb, nb):
    bsz, tp, _ = p.shape
    C, W, R = RWKV_CHUNK, RWKV_WIDTH, RWKV_HEADS * RWKV_CHUNK
    vec = lambda n: pl.BlockSpec((1, n), lambda b, c: (0, 0))
    full = lambda s: pl.BlockSpec(s, lambda b, c: (0,) * len(s))
    return pl.pallas_call(
        _rwkv_kernel,
        grid=(bsz // nb, tp // C),
        in_specs=[pl.BlockSpec((nb, C, RWKV_IN), lambda b, c: (b, c, 0)),
                  vec(RWKV_IN), vec(W), vec(W), full((128, 2 * W)), full((128, W)),
                  vec(W), vec(W), vec(W), vec(W), vec(W),
                  full((2 * W, W)), full((C, 3 * C)), full((W, W)), full((3, C, R))],
        out_specs=pl.BlockSpec((nb, C, W), lambda b, c: (b, c, 0)),
        out_shape=jax.ShapeDtypeStruct((bsz, tp, W), F32),
        scratch_shapes=[pltpu.VMEM((nb, C + 8, RWKV_IN), F32),
                        pltpu.VMEM((nb, W, W), F32)],
        compiler_params=pltpu.CompilerParams(dimension_semantics=("parallel", "arbitrary")),
        name="rwkv_mix",
    )(p, mu, w0, a0, wa2, g2, kk, ka, rk, lnw, lnb, *_rwkv_tables())


def _lru_kernel(p_ref, cw_ref, cb_ref, wax_ref, bax_ref, lam_ref, y_ref, xbuf, hprev):
    c = pl.program_id(1)
    nb = p_ref.shape[0]
    L = CHUNK
    W = LRU_WIDTH

    @pl.when(c == 0)
    def _():
        xbuf[:, 0:8, :] = jnp.zeros((nb, 8, W), F32)
        hprev[...] = jnp.zeros(hprev.shape, F32)

    rowi = lax.broadcasted_iota(jnp.int32, (L, 1), 0)
    log_a_unit = -LRU_C * _softplus(-lam_ref[...])
    for b in range(nb):
        buf = xbuf.at[b]
        buf[8:8 + L, :] = p_ref[b, :, 0:W]
        xc = _causal_conv(buf, cw_ref[...], cb_ref[...], L)
        buf[0:8, :] = buf[L:L + 8, :]
        gates = _sigmoid(_mm(xc, wax_ref[...]) + bax_ref[...])
        r, i = gates[:, 0:W], gates[:, W:2 * W]
        log_a = r * log_a_unit
        a = jnp.exp(log_a)
        t = jnp.tanh(log_a)
        one_minus_a2 = -2.0 * t / (1.0 - t)
        u = jnp.sqrt(one_minus_a2) * (i * xc)
        u = jnp.where(c * L + rowi >= PAD, u, 0.0)

        s = 1
        while s < 8:
            keep = (rowi & 7) >= s
            a_sh = jnp.where(keep, pltpu.roll(a, s, axis=0), 1.0)
            u_sh = jnp.where(keep, pltpu.roll(u, s, axis=0), 0.0)
            u = u + a * u_sh
            a = a * a_sh
            s *= 2
        carry = hprev[b, 0:1, :]
        groups = []
        for i in range(L // 8):
            h_i = u[8 * i:8 * i + 8, :] + a[8 * i:8 * i + 8, :] * carry
            groups.append(h_i)
            carry = h_i[7:8, :]
        hprev[b, 0:1, :] = carry
        h = jnp.concatenate(groups, axis=0)
        y_ref[b] = h * jax.nn.gelu(p_ref[b, :, W:2 * W], approximate=True)


def _lru_mix(p, cw, cb, wax, bax, lam, nb):
    bsz, tp, _ = p.shape
    vec = lambda n: pl.BlockSpec((1, n), lambda b, c: (0, 0))
    return pl.pallas_call(
        _lru_kernel,
        grid=(bsz // nb, tp // CHUNK),
        in_specs=[pl.BlockSpec((nb, CHUNK, LRU_IN), lambda b, c: (b, c, 0)),
                  pl.BlockSpec((LRU_CONV, LRU_WIDTH), lambda b, c: (0, 0)),
                  vec(LRU_WIDTH),
                  pl.BlockSpec((LRU_WIDTH, 2 * LRU_WIDTH), lambda b, c: (0, 0)),
                  vec(2 * LRU_WIDTH), vec(LRU_WIDTH)],
        out_specs=pl.BlockSpec((nb, CHUNK, LRU_WIDTH), lambda b, c: (b, c, 0)),
        out_shape=jax.ShapeDtypeStruct((bsz, tp, LRU_WIDTH), F32),
        scratch_shapes=[pltpu.VMEM((nb, CHUNK + 8, LRU_WIDTH), F32),
                        pltpu.VMEM((nb, 8, LRU_WIDTH), F32)],
        compiler_params=pltpu.CompilerParams(dimension_semantics=("parallel", "arbitrary")),
        name="lru_mix",
    )(p, cw, cb, wax, bax, lam)


def _ret_kernel(p_ref, cos_ref, sin_ref, dmat_ref, kdec_ref, qdec_ref, cross_ref, bd_ref, seg2_ref,
                gnw_ref, y_ref, state):
    c = pl.program_id(1)
    nb = p_ref.shape[0]
    W = RET_WIDTH

    @pl.when(c == 0)
    def _():
        state[...] = jnp.zeros(state.shape, F32)

    lane = lax.broadcasted_iota(jnp.int32, (1, RET_QK), 1)
    first_half = (lane % RET_QK_DIM) < (RET_QK_DIM // 2)
    half = RET_QK_DIM // 2
    qk_head = lane // RET_QK_DIM
    v_head = lax.broadcasted_iota(jnp.int32, (1, W), 1) // RET_V_DIM

    def rope(x):
        swapped = jnp.where(first_half, pltpu.roll(x, RET_QK - half, axis=1), pltpu.roll(x, half, axis=1))
        return x * cos_ref[...] + swapped * sin_ref[...]

    def prep(b):
        q = rope(p_ref[b, :, 0:RET_QK])
        k = rope(p_ref[b, :, RET_QK:2 * RET_QK]) * (RET_QK_DIM ** -0.5)
        v = p_ref[b, :, 2 * RET_QK:2 * RET_QK + W]
        return dict(q=q, k32=k, k=k.astype(BF16), v=v.astype(BF16), v32=v)

    def scores(d):
        d["s"] = [(_mm_nt(jnp.where(qk_head == h, d["q"], 0.0), d["k"]) * dmat_ref[h]).astype(BF16)
                  for h in range(RET_HEADS)]

    def outputs(b, d):
        y = _mm(d["q"] * qdec_ref[...], state[b])
        for h in range(RET_HEADS):
            y = y + jnp.dot(d["s"][h], jnp.where(v_head == h, d["v32"], 0.0).astype(BF16),
                            preferred_element_type=F32)
        d["y"] = y
        kv = _mm_tn(d["k32"] * kdec_ref[...], d["v"]) * bd_ref[...]
        state[b] = state[b] * cross_ref[...] + kv

    def center(d):
        d["yc"] = d["y"] - _seg_sum(d["y"], seg2_ref) * (1.0 / RET_V_DIM)

    def finish(b, d):
        var = _seg_sum(d["yc"] * d["yc"], seg2_ref) * (1.0 / RET_V_DIM)
        g = p_ref[b, :, 2 * RET_QK + W:2 * RET_QK + 2 * W]
        y_ref[b] = d["yc"] * lax.rsqrt(var + RET_GN_EPS) * gnw_ref[...] * _silu(g)

    ds = [prep(b) for b in range(nb)]
    for d in ds:
        scores(d)
    for b, d in enumerate(ds):
        outputs(b, d)
    for d in ds:
        center(d)
    for b, d in enumerate(ds):
        finish(b, d)


def _ret_mix(p, cos_t, sin_t, dmat, kdec, qdec, cross, bd, seg2, gnw, nb):
    bsz, tp, _ = p.shape
    const2 = lambda s: pl.BlockSpec(s, lambda b, c: (0, 0))
    return pl.pallas_call(
        _ret_kernel,
        grid=(bsz // nb, tp // CHUNK),
        in_specs=[pl.BlockSpec((nb, CHUNK, RET_IN), lambda b, c: (b, c, 0)),
                  pl.BlockSpec((CHUNK, RET_QK), lambda b, c: (c, 0)),
                  pl.BlockSpec((CHUNK, RET_QK), lambda b, c: (c, 0)),
                  pl.BlockSpec((RET_HEADS, CHUNK, CHUNK), lambda b, c: (0, 0, 0)),
                  const2((CHUNK, RET_QK)), const2((CHUNK, RET_QK)),
                  const2((RET_QK, RET_WIDTH)), const2((RET_QK, RET_WIDTH)),
                  const2((2 * RET_WIDTH, RET_WIDTH)), const2((1, RET_WIDTH))],
        out_specs=pl.BlockSpec((nb, CHUNK, RET_WIDTH), lambda b, c: (b, c, 0)),
        out_shape=jax.ShapeDtypeStruct((bsz, tp, RET_WIDTH), F32),
        scratch_shapes=[pltpu.VMEM((nb, RET_QK, RET_WIDTH), F32)],
        compiler_params=pltpu.CompilerParams(dimension_semantics=("parallel", "arbitrary")),
        name="ret_mix",
    )(p, cos_t, sin_t, dmat, kdec, qdec, cross, bd, seg2, gnw)


def _retention_tables(tp):
    half = RET_QK_DIM // 2
    lane = np.arange(RET_QK)
    freqs = ROPE_BASE ** (-np.arange(half, dtype=np.float64) / half)
    pos = np.arange(tp, dtype=np.float64) - PAD
    ang = pos[:, None] * freqs[lane % half][None, :]
    sign = np.where((lane % RET_QK_DIM) < half, -1.0, 1.0)
    cos_t = np.cos(ang)
    sin_t = np.sin(ang) * sign[None, :]
    log_g = np.log1p(-np.exp2(-5.0 - np.arange(RET_HEADS, dtype=np.float64)))
    idx = np.arange(CHUNK)
    rel = idx[:, None] - idx[None, :]
    dmat = np.where(rel >= 0, np.exp(np.maximum(rel, 0)[None] * log_g[:, None, None]), 0.0)
    lg_lane = log_g[lane // RET_QK_DIM]
    kdec = np.exp((CHUNK - 1 - idx)[:, None] * lg_lane[None, :])
    qdec = np.exp((idx + 1)[:, None] * lg_lane[None, :])
    cross = np.broadcast_to(np.exp(CHUNK * lg_lane)[:, None], (RET_QK, RET_WIDTH))
    v_head = np.arange(RET_WIDTH) // RET_V_DIM
    bd = (lane // RET_QK_DIM)[:, None] == v_head[None, :]
    seg = (v_head[:, None] == v_head[None, :]).astype(np.float32)
    f = lambda x: jnp.asarray(np.ascontiguousarray(x), F32)
    return (f(cos_t), f(sin_t), f(dmat), f(kdec), f(qdec), f(cross), f(bd),
            jnp.asarray(np.concatenate([seg, seg], axis=0), BF16))


def _out_ffn_kernel(h_ref, y0_ref, y1_ref, y2_ref, y3_ref, wo_ref, n1_ref, n2_ref, n3_ref,
                    wg_ref, wu_ref, wd_ref, o_ref, *, tf):
    u = None
    for i, yr in enumerate((y0_ref, y1_ref, y2_ref, y3_ref)):
        part = _mm(yr[...], wo_ref[256 * i:256 * i + 256, :])
        u = part if u is None else u + part
    h1 = h_ref[...] + _rms(u, n1_ref[...])
    hn = _rms(h1, n2_ref[...]).astype(BF16)
    acts = []
    for j in range(D_FF // tf):
        sl = slice(j * tf, (j + 1) * tf)
        act = _silu(jnp.dot(hn, wg_ref[:, sl], preferred_element_type=F32)) * \
            jnp.dot(hn, wu_ref[:, sl], preferred_element_type=F32)
        acts.append(act.astype(BF16))
    acc = jnp.dot(jnp.concatenate(acts, axis=1), wd_ref[...], preferred_element_type=F32)
    o_ref[...] = h1 + _rms(acc, n3_ref[...])


def _out_ffn(h, ys, wo, n1, n2, n3, wg, wu, wd, layer, tm, tf):
    m, d = h.shape
    row = lambda n: pl.BlockSpec((tm, n), lambda i: (i, 0))
    full = lambda s: pl.BlockSpec(s, lambda i: (0, 0))
    stacked = lambda s: pl.BlockSpec((None,) + s, lambda i: (layer, 0, 0))
    return pl.pallas_call(
        functools.partial(_out_ffn_kernel, tf=tf),
        grid=(m // tm,),
        in_specs=[row(d), row(256), row(256), row(256), row(256),
                  stacked((d, d)), full((1, d)), full((1, d)), full((1, d)),
                  stacked((d, D_FF)), stacked((d, D_FF)), stacked((D_FF, d))],
        out_specs=row(d),
        out_shape=jax.ShapeDtypeStruct((m, d), F32),
        compiler_params=pltpu.CompilerParams(
            dimension_semantics=("parallel",), vmem_limit_bytes=VMEM_LIMIT_BYTES),
        name="out_ffn",
    )(h, *ys, wo, n1, n2, n3, wg, wu, wd)


def _block_diag(blocks):
    g, n, m = blocks.shape
    eye = jnp.eye(g, dtype=blocks.dtype)
    return (eye[:, None, :, None] * blocks[:, :, None, :]).reshape(g * n, g * m)


def _row_tile(tp):
    best = 8
    for t in range(8, 1153, 8):
        if tp % t == 0:
            best = t
    return best


def kernel(x, meta_tokens, pre_mix_norm, post_mix_norm, pre_ffn_norm, post_ffn_norm, w_in, w_out, ssd_conv_w, ssd_conv_b, ssd_dt_bias, ssd_a_log, ssd_d, ssd_norm_w, rwkv_mu, rwkv_w0, rwkv_w2, rwkv_a0, rwkv_a2, rwkv_g2, rwkv_k_k, rwkv_k_a, rwkv_r_k, rwkv_ln_w, rwkv_ln_b, lru_conv_w, lru_conv_b, lru_wa, lru_ba, lru_wx, lru_bx, lru_lambda, ret_gn_w, ffn_w_gate, ffn_w_up, ffn_w_down):
    bsz, seq, d = x.shape
    depth = w_in.shape[0]
    t = N_META + seq
    tp = t + PAD
    assert d == D_MODEL and tp % CHUNK == 0
    meta = jnp.broadcast_to(meta_tokens.astype(x.dtype)[None], (bsz, N_META, d))
    h = jnp.concatenate([jnp.zeros((bsz, PAD, d), x.dtype), meta, x], axis=1)

    tm_in = _row_tile(tp)
    m_rows = bsz * tp
    tm_ffn = 512 if m_rows % 512 == 0 else CHUNK
    tf = 256
    tables = _retention_tables(tp)
    per_head = lambda v: jnp.repeat(v, SSD_HEAD_DIM, axis=-1)[None]
    r2 = lambda v: v[None]

    w_cat = _prep_w_in(w_in)
    nb = next(n for n in (8, 4, 2, 1) if bsz % n == 0)
    nb_rwkv = nb
    wo16, wg16, wu16, wd16 = (w.astype(BF16) for w in (w_out, ffn_w_gate, ffn_w_up, ffn_w_down))

    for l in range(depth):
        p_ssd, p_rwkv, p_lru, p_ret = _in_proj(h, r2(pre_mix_norm[l]), w_cat, l, tm_in)

        y_ssd = _ssd_mix(p_ssd, ssd_conv_w[l], r2(ssd_conv_b[l]), per_head(ssd_dt_bias[l]),
                         per_head(ssd_a_log[l]), per_head(ssd_d[l]), r2(ssd_norm_w[l]), nb)

        zero = jnp.zeros((64, RWKV_WIDTH), F32)
        wa2 = jnp.concatenate([jnp.concatenate([rwkv_w2[l], zero], axis=1),
                               jnp.concatenate([zero, rwkv_a2[l]], axis=1)], axis=0).astype(BF16)
        y_rwkv = _rwkv_mix(p_rwkv, r2(rwkv_mu[l]), r2(rwkv_w0[l]), r2(rwkv_a0[l]), wa2,
                           rwkv_g2[l].astype(BF16), r2(rwkv_k_k[l]), r2(rwkv_k_a[l]),
                           rwkv_r_k[l].reshape(1, RWKV_WIDTH), r2(rwkv_ln_w[l]), r2(rwkv_ln_b[l]),
                           nb_rwkv)

        wax = jnp.concatenate([_block_diag(lru_wa[l]), _block_diag(lru_wx[l])], axis=1).astype(BF16)
        bax = jnp.concatenate([lru_ba[l], lru_bx[l]])[None]
        y_lru = _lru_mix(p_lru, lru_conv_w[l], r2(lru_conv_b[l]), wax, bax, r2(lru_lambda[l]), nb)

        y_ret = _ret_mix(p_ret, *tables, r2(ret_gn_w[l]), nb)

        flat = lambda a: a.reshape(m_rows, a.shape[-1])
        h = _out_ffn(flat(h), [flat(y_ssd), flat(y_rwkv), flat(y_lru), flat(y_ret)],
                     wo16, r2(post_mix_norm[l]), r2(pre_ffn_norm[l]), r2(post_ffn_norm[l]),
                     wg16, wu16, wd16, l, tm_ffn, tf).reshape(bsz, tp, d)
    return h[:, PAD + N_META:]
```

```python
import functools
import math

import numpy as np
import jax
import jax.numpy as jnp
from jax import lax
from jax.experimental import pallas as pl
from jax.experimental.pallas import tpu as pltpu

F32 = jnp.float32
BF16 = jnp.bfloat16

D_MODEL = 1024
N_META = 16
CHUNK = 128
PAD = CHUNK - N_META
NORM_EPS = 1e-6

SSD_HEADS = 4
SSD_HEAD_DIM = 64
SSD_WIDTH = 256
SSD_STATE = 128
SSD_CONV = 4
SSD_CONV_CH = 768
SSD_IN = 1028

RWKV_HEADS = 4
RWKV_HEAD_DIM = 64
RWKV_WIDTH = 256
RWKV_IN = 1024
RWKV_GN_EPS = 64e-5
RWKV_CHUNK = 64

LRU_WIDTH = 256
LRU_CONV = 4
LRU_C = 8.0
LRU_IN = 512

RET_HEADS = 4
RET_QK_DIM = 32
RET_V_DIM = 64
RET_WIDTH = 256
RET_QK = RET_HEADS * RET_QK_DIM
RET_IN = 768
RET_GN_EPS = 1e-5
ROPE_BASE = 10000.0

D_FF = 2816

VMEM_LIMIT_BYTES = 52 * 1024 * 1024

P_SSD_W = SSD_WIDTH + SSD_CONV_CH + SSD_WIDTH
IN_COLS = P_SSD_W + RWKV_IN + LRU_IN + RET_IN


def _mm(a, b):
    return jnp.dot(a.astype(BF16), b.astype(BF16), preferred_element_type=F32)


def _mm_nt(a, b):
    return lax.dot_general(a.astype(BF16), b.astype(BF16), (((1,), (1,)), ((), ())),
                           preferred_element_type=F32)


def _mm_tn(a, b):
    return lax.dot_general(a.astype(BF16), b.astype(BF16), (((0,), (0,)), ((), ())),
                           preferred_element_type=F32)


def _mm_exact(a, b):
    return jnp.dot(a, b, preferred_element_type=F32, precision=lax.Precision.HIGHEST)


def _sigmoid(x):
    return 1.0 / (1.0 + jnp.exp(-x))


def _silu(x):
    return x * _sigmoid(x)


def _softplus(x):
    return jnp.maximum(x, 0.0) + jnp.log1p(jnp.exp(-jnp.abs(x)))


def _rms(x, w):
    return x * lax.rsqrt(jnp.mean(x * x, axis=-1, keepdims=True) + NORM_EPS) * w


def _causal_conv(buf, w, b, rows):
    acc = b + w[3:4, :] * buf[8:8 + rows, :]
    acc = acc + w[2:3, :] * buf[7:7 + rows, :]
    acc = acc + w[1:2, :] * buf[6:6 + rows, :]
    acc = acc + w[0:1, :] * buf[5:5 + rows, :]
    return acc


def _prep_w_in_kernel(w_ref, dt_ref, o_ref):
    main = SSD_WIDTH + SSD_CONV_CH
    o_ref[:, 0:main] = w_ref[:, 0:main]
    o_ref[:, main:P_SSD_W] = dt_ref[...]
    o_ref[:, P_SSD_W:] = w_ref[:, SSD_IN:]


def _prep_w_in(w_in):
    depth, d, n = w_in.shape
    tr = 256
    main = SSD_WIDTH + SSD_CONV_CH
    w16 = w_in.astype(BF16)
    dt_rep = jnp.repeat(w_in[:, :, main:SSD_IN], SSD_HEAD_DIM, axis=2).astype(BF16)
    return pl.pallas_call(
        _prep_w_in_kernel,
        grid=(depth, d // tr),
        in_specs=[pl.BlockSpec((None, tr, n), lambda l, i: (l, i, 0)),
                  pl.BlockSpec((None, tr, SSD_WIDTH), lambda l, i: (l, i, 0))],
        out_specs=pl.BlockSpec((None, tr, IN_COLS), lambda l, i: (l, i, 0)),
        out_shape=jax.ShapeDtypeStruct((depth, d, IN_COLS), BF16),
        compiler_params=pltpu.CompilerParams(dimension_semantics=("parallel", "parallel")),
        name="prep_w_in",
    )(w16, dt_rep)


def _in_proj_kernel(h_ref, nw_ref, w_ref, ssd_ref, rwkv_ref, lru_ref, ret_ref):
    tm = h_ref.shape[0]
    row = pl.program_id(1) * tm + lax.broadcasted_iota(jnp.int32, (tm, 1), 0)
    hn = _rms(h_ref[...], nw_ref[...])
    hn = jnp.where(row >= PAD, hn, 0.0).astype(BF16)
    o = 0
    for ref in (ssd_ref, rwkv_ref, lru_ref, ret_ref):
        n = ref.shape[1]
        ref[...] = jnp.dot(hn, w_ref[:, o:o + n], preferred_element_type=F32)
        o += n


def _in_proj(h, nw, w, layer, tm):
    bsz, tp, d = h.shape
    widths = (P_SSD_W, RWKV_IN, LRU_IN, RET_IN)
    return pl.pallas_call(
        _in_proj_kernel,
        grid=(bsz, tp // tm),
        in_specs=[pl.BlockSpec((None, tm, d), lambda b, i: (b, i, 0)),
                  pl.BlockSpec((1, d), lambda b, i: (0, 0)),
                  pl.BlockSpec((None, d, IN_COLS), lambda b, i: (layer, 0, 0))],
        out_specs=[pl.BlockSpec((None, tm, n), lambda b, i: (b, i, 0)) for n in widths],
        out_shape=[jax.ShapeDtypeStruct((bsz, tp, n), F32) for n in widths],
        compiler_params=pltpu.CompilerParams(
            dimension_semantics=("parallel", "arbitrary"), vmem_limit_bytes=VMEM_LIMIT_BYTES),
        name="in_proj",
    )(h, nw, w)


def _split_bf16(x, parts):
    out = []
    for _ in range(parts):
        hi = x.astype(BF16)
        out.append(hi)
        x = x - hi.astype(F32)
    return out


def _cumsum_rows(x, tril3_ref):
    return jnp.dot(tril3_ref[...], jnp.concatenate(_split_bf16(x, 3), axis=0), preferred_element_type=F32)


def _tril3(n):
    tril = (np.arange(n)[:, None] >= np.arange(n)[None, :]).astype(np.float32)
    return jnp.asarray(np.concatenate([tril] * 3, axis=1), BF16)


def _ssd_kernel(p_ref, cw_ref, cb_ref, dtb_ref, alog_ref, dsk_ref, nw_ref, tril3_ref, y_ref, xbuf, state):
    c = pl.program_id(1)
    nb = p_ref.shape[0]
    L = CHUNK

    @pl.when(c == 0)
    def _():
        xbuf[:, 0:8, :] = jnp.zeros((nb, 8, SSD_CONV_CH), F32)
        state[...] = jnp.zeros(state.shape, F32)

    row = c * L + lax.broadcasted_iota(jnp.int32, (L, 1), 0)
    causal = lax.broadcasted_iota(jnp.int32, (L, L), 0) >= lax.broadcasted_iota(jnp.int32, (L, L), 1)
    lane_lo = lax.broadcasted_iota(jnp.int32, (1, 128), 1) < SSD_HEAD_DIM
    groups = [slice(128 * g, 128 * g + 128) for g in range(2)]

    def prep(b):
        buf = xbuf.at[b]
        buf[8:8 + L, :] = p_ref[b, :, SSD_WIDTH:SSD_WIDTH + SSD_CONV_CH]
        xbc = _silu(_causal_conv(buf, cw_ref[...], cb_ref[...], L))
        buf[0:8, :] = buf[L:L + 8, :]
        dt = _softplus(p_ref[b, :, SSD_WIDTH + SSD_CONV_CH:P_SSD_W] + dtb_ref[...])
        dt = jnp.where(row >= PAD, dt, 0.0)
        a = dt * (-jnp.exp(alog_ref[...]))
        xs = xbc[:, 0:256]
        return dict(xs=xs, bs=xbc[:, 256:512], cs=xbc[:, 512:768], xdt=xs * dt,
                    acs=_cumsum_rows(a, tril3_ref), y=[None, None])

    def gram(d, g):
        sl = groups[g]
        return _mm_nt(d["cs"][:, sl], d["bs"][:, sl])

    def chunk(b, d, g, gmat):
        sl = groups[g]
        acs_g = d["acs"][:, sl]
        acs_t = acs_g.T
        a_last = acs_g[L - 1:L, :]
        b_g, c_g, x_g = d["bs"][:, sl], d["cs"][:, sl], d["xdt"][:, sl]
        ms = []
        for hh in range(2):
            o = SSD_HEAD_DIM * hh
            diff = acs_g[:, o:o + 1] - acs_t[o:o + 1, :]
            ms.append(gmat * jnp.exp(jnp.where(causal, diff, -1e30)))
        mcat = jnp.concatenate(ms, axis=1)
        xbd = jnp.concatenate([jnp.where(lane_lo, x_g, 0.0), jnp.where(lane_lo, 0.0, x_g)], axis=0)
        s_in = state[b, g]
        d["y"][g] = _mm(mcat, xbd) + _mm(c_g, s_in) * jnp.exp(acs_g)
        state[b, g] = s_in * jnp.exp(a_last) + _mm_tn(b_g, x_g * jnp.exp(a_last - acs_g))

    def finish(b, d, g):
        sl = groups[g]
        y_g = (d["y"][g] + dsk_ref[:, sl] * d["xs"][:, sl]) * _silu(p_ref[b, :, sl])
        y_ref[b, :, sl] = _rms(y_g, nw_ref[:, sl])

    ds = [prep(b) for b in range(nb)]
    gm = [[gram(d, g) for g in range(2)] for d in ds]
    for b, d in enumerate(ds):
        for g in range(2):
            chunk(b, d, g, gm[b][g])
    for b, d in enumerate(ds):
        for g in range(2):
            finish(b, d, g)


def _ssd_mix(p, cw, cb, dtb, alog, dsk, nw, nb):
    bsz, tp, _ = p.shape
    vec = lambda n: pl.BlockSpec((1, n), lambda b, c: (0, 0))
    return pl.pallas_call(
        _ssd_kernel,
        grid=(bsz // nb, tp // CHUNK),
        in_specs=[pl.BlockSpec((nb, CHUNK, P_SSD_W), lambda b, c: (b, c, 0)),
                  pl.BlockSpec((SSD_CONV, SSD_CONV_CH), lambda b, c: (0, 0)),
                  vec(SSD_CONV_CH), vec(SSD_WIDTH), vec(SSD_WIDTH), vec(SSD_WIDTH), vec(SSD_WIDTH),
                  pl.BlockSpec((CHUNK, 3 * CHUNK), lambda b, c: (0, 0))],
        out_specs=pl.BlockSpec((nb, CHUNK, SSD_WIDTH), lambda b, c: (b, c, 0)),
        out_shape=jax.ShapeDtypeStruct((bsz, tp, SSD_WIDTH), F32),
        scratch_shapes=[pltpu.VMEM((nb, CHUNK + 8, SSD_CONV_CH), F32),
                        pltpu.VMEM((nb, 2, SSD_STATE, 128), F32)],
        compiler_params=pltpu.CompilerParams(dimension_semantics=("parallel", "arbitrary")),
        name="ssd_mix",
    )(p, cw, cb, dtb, alog, dsk, nw, _tril3(CHUNK))


def _seg_sum(x, seg2_ref):
    return jnp.dot(jnp.concatenate(_split_bf16(x, 2), axis=1), seg2_ref[...], preferred_element_type=F32)


def _rwkv_kernel(p_ref, mu_ref, w0_ref, a0_ref, wa2_ref, g2_ref, kk_ref, ka_ref, rk_ref,
                 lnw_ref, lnb_ref, seg2_ref, tril3_ref, mask_ref, maskw_ref, y_ref, pbuf, state):
    c = pl.program_id(1)
    nb = p_ref.shape[0]
    C = RWKV_CHUNK
    W = RWKV_WIDTH
    R = RWKV_HEADS * C

    @pl.when(c == 0)
    def _():
        pbuf[:, 0:8, :] = jnp.zeros((nb, 8, RWKV_IN), F32)
        state[...] = jnp.zeros(state.shape, F32)

    lane = lax.broadcasted_iota(jnp.int32, (1, 128), 1)
    lane_head = lax.broadcasted_iota(jnp.int32, (1, W), 1) // RWKV_HEAD_DIM

    def stack(x):
        x = x.astype(BF16)
        zero = jnp.zeros_like(x)
        return jnp.concatenate([jnp.where(lane_head == h, x, zero) for h in range(RWKV_HEADS)], axis=0)

    def prep(b):
        pbuf[b, 8:8 + C, :] = p_ref[b]
        p = pbuf[b, 8:8 + C, :]
        p = p + (pbuf[b, 7:7 + C, :] - p) * mu_ref[...]
        pbuf[b, 0:8, :] = pbuf[b, C:C + 8, :]
        r, k, v = p[:, 0:W], p[:, W:2 * W], p[:, 2 * W:3 * W]
        lat = p[:, 3 * W:3 * W + 128]
        lat = jnp.where(lane < 64, jnp.tanh(lat), lat)
        wa = _mm(lat, wa2_ref[...])
        logw = -math.exp(-0.5) * _sigmoid(w0_ref[...] + wa[:, 0:W])
        a = _sigmoid(a0_ref[...] + wa[:, W:2 * W])
        g = _mm(_sigmoid(p[:, 3 * W + 128:3 * W + 256]), g2_ref[...])
        kk = k * kk_ref[...]
        kk = kk / jnp.maximum(jnp.sqrt(_seg_sum(kk * kk, seg2_ref)), 1e-12)
        k2 = k * (1.0 + (a - 1.0) * ka_ref[...])
        bonus = _seg_sum(r * k2 * rk_ref[...], seg2_ref) * v
        cum = _cumsum_rows(logw, tril3_ref)
        e_neg = jnp.exp(-cum)
        pc = jnp.exp(cum[C - 1:C, :])
        bt = kk * a * e_neg
        kt = k2 * e_neg
        return dict(
            lhs=jnp.concatenate([-kk * jnp.exp(cum - logw), r * jnp.exp(cum)], axis=0).astype(BF16),
            rhs=jnp.concatenate([stack(bt), stack(kt)], axis=0),
            upd=jnp.concatenate([bt * pc, kt * pc], axis=0).astype(BF16),
            v=v.astype(BF16), v_bd=stack(v), pc=pc, bonus=bonus, g=g)

    def scores(d):
        sc = lax.dot_general(d["lhs"], d["rhs"], (((1,), (1,)), ((), ())), preferred_element_type=F32)
        strict_w, incl_w = maskw_ref[0], maskw_ref[1]
        d["npow"] = sc[0:C, 0:R] * strict_w
        d["n_bd"] = stack(d["npow"])
        d["a_k"] = jnp.concatenate([sc[0:C, R:2 * R] * strict_w, sc[C:2 * C, R:2 * R] * incl_w],
                                   axis=0).astype(BF16)
        d["a_rb"] = (sc[C:2 * C, 0:R] * incl_w).astype(BF16)
        d["tinv"] = maskw_ref[2] + d["npow"]

    def double(d):
        d["npow"] = jnp.dot(d["npow"].astype(BF16), d["n_bd"], preferred_element_type=F32)
        d["n_bd"] = stack(d["npow"])
        d["tinv"] = d["tinv"] + jnp.dot(d["tinv"].astype(BF16), d["n_bd"], preferred_element_type=F32)

    def read_state(b, d):
        d["sxav"] = _mm_nt(d["lhs"], state[b]) + jnp.dot(d["a_k"], d["v_bd"], preferred_element_type=F32)

    def correction(d):
        d["u"] = jnp.dot(d["tinv"].astype(BF16), stack(d["sxav"][0:C]), preferred_element_type=F32)

    def write_state(b, d):
        u = d["u"]
        d["y"] = d["sxav"][C:2 * C] + jnp.dot(d["a_rb"], stack(u), preferred_element_type=F32)
        new = _mm_tn(jnp.concatenate([u.astype(BF16), d["v"]], axis=0), d["upd"])
        state[b] = state[b] * d["pc"] + new * mask_ref[...]

    def finish(b, d):
        inv_n = 1.0 / RWKV_HEAD_DIM
        yc = d["y"] - _seg_sum(d["y"], seg2_ref) * inv_n
        var = _seg_sum(yc * yc, seg2_ref) * inv_n
        yn = yc * lax.rsqrt(var + RWKV_GN_EPS) * lnw_ref[...] + lnb_ref[...]
        y_ref[b] = (yn + d["bonus"]) * d["g"]

    ds = [prep(b) for b in range(nb)]
    for d in ds:
        scores(d)
    n = 1
    while 2 * n < C:
        for d in ds:
            double(d)
        n *= 2
    for b, d in enumerate(ds):
        read_state(b, d)
    for d in ds:
        correction(d)
    for b, d in enumerate(ds):
        write_state(b, d)
    for b, d in enumerate(ds):
        finish(b, d)


def _rwkv_tables():
    C, W, R = RWKV_CHUNK, RWKV_WIDTH, RWKV_HEADS * RWKV_CHUNK
    head = np.arange(W) // RWKV_HEAD_DIM
    seg = (head[:, None] == head[None, :]).astype(np.float32)
    assert C == RWKV_HEAD_DIM
    cc = np.arange(R)[None, :]
    tt = np.arange(C)[:, None]
    wide = np.stack([(cc % C) < tt, (cc % C) <= tt, (cc % C) == tt])
    return (jnp.asarray(np.concatenate([seg, seg], axis=0), BF16), _tril3(C), jnp.asarray(seg, F32),
            jnp.asarray(wide.astype(np.float32), F32))


def _rwkv_mix(p, mu, w0, a0, wa2, g2, kk, ka, rk, lnw, lnb, nb):
    bsz, tp, _ = p.shape
    C, W, R = RWKV_CHUNK, RWKV_WIDTH, RWKV_HEADS * RWKV_CHUNK
    vec = lambda n: pl.BlockSpec((1, n), lambda b, c: (0, 0))
    full = lambda s: pl.BlockSpec(s, lambda b, c: (0,) * len(s))
    return pl.pallas_call(
        _rwkv_kernel,
        grid=(bsz // nb, tp // C),
        in_specs=[pl.BlockSpec((nb, C, RWKV_IN), lambda b, c: (b, c, 0)),
                  vec(RWKV_IN), vec(W), vec(W), full((128, 2 * W)), full((128, W)),
                  vec(W), vec(W), vec(W), vec(W), vec(W),
                  full((2 * W, W)), full((C, 3 * C)), full((W, W)), full((3, C, R))],
        out_specs=pl.BlockSpec((nb, C, W), lambda b, c: (b, c, 0)),
        out_shape=jax.ShapeDtypeStruct((bsz, tp, W), F32),
        scratch_shapes=[pltpu.VMEM((nb, C + 8, RWKV_IN), F32),
                        pltpu.VMEM((nb, W, W), F32)],
        compiler_params=pltpu.CompilerParams(dimension_semantics=("parallel", "arbitrary")),
        name="rwkv_mix",
    )(p, mu, w0, a0, wa2, g2, kk, ka, rk, lnw, lnb, *_rwkv_tables())


def _lru_kernel(p_ref, cw_ref, cb_ref, wax_ref, bax_ref, lam_ref, y_ref, xbuf, hprev):
    c = pl.program_id(1)
    nb = p_ref.shape[0]
    L = CHUNK
    W = LRU_WIDTH

    @pl.when(c == 0)
    def _():
        xbuf[:, 0:8, :] = jnp.zeros((nb, 8, W), F32)
        hprev[...] = jnp.zeros(hprev.shape, F32)

    rowi = lax.broadcasted_iota(jnp.int32, (L, 1), 0)
    log_a_unit = -LRU_C * _softplus(-lam_ref[...])
    for b in range(nb):
        buf = xbuf.at[b]
        buf[8:8 + L, :] = p_ref[b, :, 0:W]
        xc = _causal_conv(buf, cw_ref[...], cb_ref[...], L)
        buf[0:8, :] = buf[L:L + 8, :]
        gates = _sigmoid(_mm(xc, wax_ref[...]) + bax_ref[...])
        r, i = gates[:, 0:W], gates[:, W:2 * W]
        log_a = r * log_a_unit
        a = jnp.exp(log_a)
        t = jnp.tanh(log_a)
        one_minus_a2 = -2.0 * t / (1.0 - t)
        u = jnp.sqrt(one_minus_a2) * (i * xc)
        u = jnp.where(c * L + rowi >= PAD, u, 0.0)

        s = 1
        while s < 8:
            keep = (rowi & 7) >= s
            a_sh = jnp.where(keep, pltpu.roll(a, s, axis=0), 1.0)
            u_sh = jnp.where(keep, pltpu.roll(u, s, axis=0), 0.0)
            u = u + a * u_sh
            a = a * a_sh
            s *= 2
        carry = hprev[b, 0:1, :]
        groups = []
        for i in range(L // 8):
            h_i = u[8 * i:8 * i + 8, :] + a[8 * i:8 * i + 8, :] * carry
            groups.append(h_i)
            carry = h_i[7:8, :]
        hprev[b, 0:1, :] = carry
        h = jnp.concatenate(groups, axis=0)
        y_ref[b] = h * jax.nn.gelu(p_ref[b, :, W:2 * W], approximate=True)


def _lru_mix(p, cw, cb, wax, bax, lam, nb):
    bsz, tp, _ = p.shape
    vec = lambda n: pl.BlockSpec((1, n), lambda b, c: (0, 0))
    return pl.pallas_call(
        _lru_kernel,
        grid=(bsz // nb, tp // CHUNK),
        in_specs=[pl.BlockSpec((nb, CHUNK, LRU_IN), lambda b, c: (b, c, 0)),
                  pl.BlockSpec((LRU_CONV, LRU_WIDTH), lambda b, c: (0, 0)),
                  vec(LRU_WIDTH),
                  pl.BlockSpec((LRU_WIDTH, 2 * LRU_WIDTH), lambda b, c: (0, 0)),
                  vec(2 * LRU_WIDTH), vec(LRU_WIDTH)],
        out_specs=pl.BlockSpec((nb, CHUNK, LRU_WIDTH), lambda b, c: (b, c, 0)),
        out_shape=jax.ShapeDtypeStruct((bsz, tp, LRU_WIDTH), F32),
        scratch_shapes=[pltpu.VMEM((nb, CHUNK + 8, LRU_WIDTH), F32),
                        pltpu.VMEM((nb, 8, LRU_WIDTH), F32)],
        compiler_params=pltpu.CompilerParams(dimension_semantics=("parallel", "arbitrary")),
        name="lru_mix",
    )(p, cw, cb, wax, bax, lam)


def _ret_kernel(p_ref, cos_ref, sin_ref, dmat_ref, kdec_ref, qdec_ref, cross_ref, bd_ref, seg2_ref,
                gnw_ref, y_ref, state):
    c = pl.program_id(1)
    nb = p_ref.shape[0]
    W = RET_WIDTH

    @pl.when(c == 0)
    def _():
        state[...] = jnp.zeros(state.shape, F32)

    lane = lax.broadcasted_iota(jnp.int32, (1, RET_QK), 1)
    first_half = (lane % RET_QK_DIM) < (RET_QK_DIM // 2)
    half = RET_QK_DIM // 2
    qk_head = lane // RET_QK_DIM
    v_head = lax.broadcasted_iota(jnp.int32, (1, W), 1) // RET_V_DIM

    def rope(x):
        swapped = jnp.where(first_half, pltpu.roll(x, RET_QK - half, axis=1), pltpu.roll(x, half, axis=1))
        return x * cos_ref[...] + swapped * sin_ref[...]

    def prep(b):
        q = rope(p_ref[b, :, 0:RET_QK])
        k = rope(p_ref[b, :, RET_QK:2 * RET_QK]) * (RET_QK_DIM ** -0.5)
        v = p_ref[b, :, 2 * RET_QK:2 * RET_QK + W]
        return dict(q=q, k32=k, k=k.astype(BF16), v=v.astype(BF16), v32=v)

    def scores(d):
        d["s"] = [(_mm_nt(jnp.where(qk_head == h, d["q"], 0.0), d["k"]) * dmat_ref[h]).astype(BF16)
                  for h in range(RET_HEADS)]

    def outputs(b, d):
        y = _mm(d["q"] * qdec_ref[...], state[b])
        for h in range(RET_HEADS):
            y = y + jnp.dot(d["s"][h], jnp.where(v_head == h, d["v32"], 0.0).astype(BF16),
                            preferred_element_type=F32)
        d["y"] = y
        kv = _mm_tn(d["k32"] * kdec_ref[...], d["v"]) * bd_ref[...]
        state[b] = state[b] * cross_ref[...] + kv

    def center(d):
        d["yc"] = d["y"] - _seg_sum(d["y"], seg2_ref) * (1.0 / RET_V_DIM)

    def finish(b, d):
        var = _seg_sum(d["yc"] * d["yc"], seg2_ref) * (1.0 / RET_V_DIM)
        g = p_ref[b, :, 2 * RET_QK + W:2 * RET_QK + 2 * W]
        y_ref[b] = d["yc"] * lax.rsqrt(var + RET_GN_EPS) * gnw_ref[...] * _silu(g)

    ds = [prep(b) for b in range(nb)]
    for d in ds:
        scores(d)
    for b, d in enumerate(ds):
        outputs(b, d)
    for d in ds:
        center(d)
    for b, d in enumerate(ds):
        finish(b, d)


def _ret_mix(p, cos_t, sin_t, dmat, kdec, qdec, cross, bd, seg2, gnw, nb):
    bsz, tp, _ = p.shape
    const2 = lambda s: pl.BlockSpec(s, lambda b, c: (0, 0))
    return pl.pallas_call(
        _ret_kernel,
        grid=(bsz // nb, tp // CHUNK),
        in_specs=[pl.BlockSpec((nb, CHUNK, RET_IN), lambda b, c: (b, c, 0)),
                  pl.BlockSpec((CHUNK, RET_QK), lambda b, c: (c, 0)),
                  pl.BlockSpec((CHUNK, RET_QK), lambda b, c: (c, 0)),
                  pl.BlockSpec((RET_HEADS, CHUNK, CHUNK), lambda b, c: (0, 0, 0)),
                  const2((CHUNK, RET_QK)), const2((CHUNK, RET_QK)),
                  const2((RET_QK, RET_WIDTH)), const2((RET_QK, RET_WIDTH)),
                  const2((2 * RET_WIDTH, RET_WIDTH)), const2((1, RET_WIDTH))],
        out_specs=pl.BlockSpec((nb, CHUNK, RET_WIDTH), lambda b, c: (b, c, 0)),
        out_shape=jax.ShapeDtypeStruct((bsz, tp, RET_WIDTH), F32),
        scratch_shapes=[pltpu.VMEM((nb, RET_QK, RET_WIDTH), F32)],
        compiler_params=pltpu.CompilerParams(dimension_semantics=("parallel", "arbitrary")),
        name="ret_mix",
    )(p, cos_t, sin_t, dmat, kdec, qdec, cross, bd, seg2, gnw)


def _retention_tables(tp):
    half = RET_QK_DIM // 2
    lane = np.arange(RET_QK)
    freqs = ROPE_BASE ** (-np.arange(half, dtype=np.float64) / half)
    pos = np.arange(tp, dtype=np.float64) - PAD
    ang = pos[:, None] * freqs[lane % half][None, :]
    sign = np.where((lane % RET_QK_DIM) < half, -1.0, 1.0)
    cos_t = np.cos(ang)
    sin_t = np.sin(ang) * sign[None, :]
    log_g = np.log1p(-np.exp2(-5.0 - np.arange(RET_HEADS, dtype=np.float64)))
    idx = np.arange(CHUNK)
    rel = idx[:, None] - idx[None, :]
    dmat = np.where(rel >= 0, np.exp(np.maximum(rel, 0)[None] * log_g[:, None, None]), 0.0)
    lg_lane = log_g[lane // RET_QK_DIM]
    kdec = np.exp((CHUNK - 1 - idx)[:, None] * lg_lane[None, :])
    qdec = np.exp((idx + 1)[:, None] * lg_lane[None, :])
    cross = np.broadcast_to(np.exp(CHUNK * lg_lane)[:, None], (RET_QK, RET_WIDTH))
    v_head = np.arange(RET_WIDTH) // RET_V_DIM
    bd = (lane // RET_QK_DIM)[:, None] == v_head[None, :]
    seg = (v_head[:, None] == v_head[None, :]).astype(np.float32)
    f = lambda x: jnp.asarray(np.ascontiguousarray(x), F32)
    return (f(cos_t), f(sin_t), f(dmat), f(kdec), f(qdec), f(cross), f(bd),
            jnp.asarray(np.concatenate([seg, seg], axis=0), BF16))


def _out_ffn_kernel(h_ref, y0_ref, y1_ref, y2_ref, y3_ref, wo_ref, n1_ref, n2_ref, n3_ref,
                    wg_ref, wu_ref, wd_ref, o_ref, *, tf):
    u = None
    for i, yr in enumerate((y0_ref, y1_ref, y2_ref, y3_ref)):
        part = _mm(yr[...], wo_ref[256 * i:256 * i + 256, :])
        u = part if u is None else u + part
    h1 = h_ref[...] + _rms(u, n1_ref[...])
    hn = _rms(h1, n2_ref[...]).astype(BF16)
    acts = []
    for j in range(D_FF // tf):
        sl = slice(j * tf, (j + 1) * tf)
        act = _silu(jnp.dot(hn, wg_ref[:, sl], preferred_element_type=F32)) * \
            jnp.dot(hn, wu_ref[:, sl], preferred_element_type=F32)
        acts.append(act.astype(BF16))
    acc = jnp.dot(jnp.concatenate(acts, axis=1), wd_ref[...], preferred_element_type=F32)
    o_ref[...] = h1 + _rms(acc, n3_ref[...])


def _out_ffn(h, ys, wo, n1, n2, n3, wg, wu, wd, layer, tm, tf):
    m, d = h.shape
    row = lambda n: pl.BlockSpec((tm, n), lambda i: (i, 0))
    full = lambda s: pl.BlockSpec(s, lambda i: (0, 0))
    stacked = lambda s: pl.BlockSpec((None,) + s, lambda i: (layer, 0, 0))
    return pl.pallas_call(
        functools.partial(_out_ffn_kernel, tf=tf),
        grid=(m // tm,),
        in_specs=[row(d), row(256), row(256), row(256), row(256),
                  stacked((d, d)), full((1, d)), full((1, d)), full((1, d)),
                  stacked((d, D_FF)), stacked((d, D_FF)), stacked((D_FF, d))],
        out_specs=row(d),
        out_shape=jax.ShapeDtypeStruct((m, d), F32),
        compiler_params=pltpu.CompilerParams(
            dimension_semantics=("parallel",), vmem_limit_bytes=VMEM_LIMIT_BYTES),
        name="out_ffn",
    )(h, *ys, wo, n1, n2, n3, wg, wu, wd)


def _block_diag(blocks):
    g, n, m = blocks.shape
    eye = jnp.eye(g, dtype=blocks.dtype)
    return (eye[:, None, :, None] * blocks[:, :, None, :]).reshape(g * n, g * m)


def _row_tile(tp):
    best = 8
    for t in range(8, 1153, 8):
        if tp % t == 0:
            best = t
    return best


def kernel(x, meta_tokens, pre_mix_norm, post_mix_norm, pre_ffn_norm, post_ffn_norm, w_in, w_out, ssd_conv_w, ssd_conv_b, ssd_dt_bias, ssd_a_log, ssd_d, ssd_norm_w, rwkv_mu, rwkv_w0, rwkv_w2, rwkv_a0, rwkv_a2, rwkv_g2, rwkv_k_k, rwkv_k_a, rwkv_r_k, rwkv_ln_w, rwkv_ln_b, lru_conv_w, lru_conv_b, lru_wa, lru_ba, lru_wx, lru_bx, lru_lambda, ret_gn_w, ffn_w_gate, ffn_w_up, ffn_w_down):
    bsz, seq, d = x.shape
    depth = w_in.shape[0]
    t = N_META + seq
    tp = t + PAD
    assert d == D_MODEL and tp % CHUNK == 0
    meta = jnp.broadcast_to(meta_tokens.astype(x.dtype)[None], (bsz, N_META, d))
    h = jnp.concatenate([jnp.zeros((bsz, PAD, d), x.dtype), meta, x], axis=1)

    tm_in = _row_tile(tp)
    m_rows = bsz * tp
    tm_ffn = 512 if m_rows % 512 == 0 else CHUNK
    tf = 256
    tables = _retention_tables(tp)
    per_head = lambda v: jnp.repeat(v, SSD_HEAD_DIM, axis=-1)[None]
    r2 = lambda v: v[None]

    w_cat = _prep_w_in(w_in)
    nb = next(n for n in (8, 4, 2, 1) if bsz % n == 0)
    nb_rwkv = nb
    wo16, wg16, wu16, wd16 = (w.astype(BF16) for w in (w_out, ffn_w_gate, ffn_w_up, ffn_w_down))

    for l in range(depth):
        p_ssd, p_rwkv, p_lru, p_ret = _in_proj(h, r2(pre_mix_norm[l]), w_cat, l, tm_in)

        y_ssd = _ssd_mix(p_ssd, ssd_conv_w[l], r2(ssd_conv_b[l]), per_head(ssd_dt_bias[l]),
                         per_head(ssd_a_log[l]), per_head(ssd_d[l]), r2(ssd_norm_w[l]), nb)

        zero = jnp.zeros((64, RWKV_WIDTH), F32)
        wa2 = jnp.concatenate([jnp.concatenate([rwkv_w2[l], zero], axis=1),
                               jnp.concatenate([zero, rwkv_a2[l]], axis=1)], axis=0).astype(BF16)
        y_rwkv = _rwkv_mix(p_rwkv, r2(rwkv_mu[l]), r2(rwkv_w0[l]), r2(rwkv_a0[l]), wa2,
                           rwkv_g2[l].astype(BF16), r2(rwkv_k_k[l]), r2(rwkv_k_a[l]),
                           rwkv_r_k[l].reshape(1, RWKV_WIDTH), r2(rwkv_ln_w[l]), r2(rwkv_ln_b[l]),
                           nb_rwkv)

        wax = jnp.concatenate([_block_diag(lru_wa[l]), _block_diag(lru_wx[l])], axis=1).astype(BF16)
        bax = jnp.concatenate([lru_ba[l], lru_bx[l]])[None]
        y_lru = _lru_mix(p_lru, lru_conv_w[l], r2(lru_conv_b[l]), wax, bax, r2(lru_lambda[l]), nb)

        y_ret = _ret_mix(p_ret, *tables, r2(ret_gn_w[l]), nb)

        flat = lambda a: a.reshape(m_rows, a.shape[-1])
        h = _out_ffn(flat(h), [flat(y_ssd), flat(y_rwkv), flat(y_lru), flat(y_ret)],
                     wo16, r2(post_mix_norm[l]), r2(pre_ffn_norm[l]), r2(post_ffn_norm[l]),
                     wg16, wu16, wd16, l, tm_ffn, tf).reshape(bsz, tp, d)
    return h[:, PAD + N_META:]
```

```python
import functools
import math

import numpy as np
import jax
import jax.numpy as jnp
from jax import lax
from jax.experimental import pallas as pl
from jax.experimental.pallas import tpu as pltpu

F32 = jnp.float32
BF16 = jnp.bfloat16

D_MODEL = 1024
N_META = 16
CHUNK = 128
PAD = CHUNK - N_META
NORM_EPS = 1e-6

SSD_HEADS = 4
SSD_HEAD_DIM = 64
SSD_WIDTH = 256
SSD_STATE = 128
SSD_CONV = 4
SSD_CONV_CH = 768
SSD_IN = 1028

RWKV_HEADS = 4
RWKV_HEAD_DIM = 64
RWKV_WIDTH = 256
RWKV_IN = 1024
RWKV_GN_EPS = 64e-5
RWKV_CHUNK = 64

LRU_WIDTH = 256
LRU_CONV = 4
LRU_C = 8.0
LRU_IN = 512

RET_HEADS = 4
RET_QK_DIM = 32
RET_V_DIM = 64
RET_WIDTH = 256
RET_QK = RET_HEADS * RET_QK_DIM
RET_IN = 768
RET_GN_EPS = 1e-5
ROPE_BASE = 10000.0

D_FF = 2816

VMEM_LIMIT_BYTES = 52 * 1024 * 1024

P_SSD_W = SSD_WIDTH + SSD_CONV_CH + SSD_WIDTH
IN_COLS = P_SSD_W + RWKV_IN + LRU_IN + RET_IN


def _mm(a, b):
    return jnp.dot(a.astype(BF16), b.astype(BF16), preferred_element_type=F32)


def _mm_nt(a, b):
    return lax.dot_general(a.astype(BF16), b.astype(BF16), (((1,), (1,)), ((), ())),
                           preferred_element_type=F32)


def _mm_tn(a, b):
    return lax.dot_general(a.astype(BF16), b.astype(BF16), (((0,), (0,)), ((), ())),
                           preferred_element_type=F32)


def _mm_exact(a, b):
    return jnp.dot(a, b, preferred_element_type=F32, precision=lax.Precision.HIGHEST)


def _sigmoid(x):
    return 1.0 / (1.0 + jnp.exp(-x))


def _silu(x):
    return x * _sigmoid(x)


def _softplus(x):
    return jnp.maximum(x, 0.0) + jnp.log1p(jnp.exp(-jnp.abs(x)))


def _rms(x, w):
    return x * lax.rsqrt(jnp.mean(x * x, axis=-1, keepdims=True) + NORM_EPS) * w


def _causal_conv(buf, w, b, rows):
    acc = b + w[3:4, :] * buf[8:8 + rows, :]
    acc = acc + w[2:3, :] * buf[7:7 + rows, :]
    acc = acc + w[1:2, :] * buf[6:6 + rows, :]
    acc = acc + w[0:1, :] * buf[5:5 + rows, :]
    return acc


def _prep_w_in_kernel(w_ref, dt_ref, o_ref):
    main = SSD_WIDTH + SSD_CONV_CH
    o_ref[:, 0:main] = w_ref[:, 0:main]
    o_ref[:, main:P_SSD_W] = dt_ref[...]
    o_ref[:, P_SSD_W:] = w_ref[:, SSD_IN:]


def _prep_w_in(w_in):
    depth, d, n = w_in.shape
    tr = 256
    main = SSD_WIDTH + SSD_CONV_CH
    w16 = w_in.astype(BF16)
    dt_rep = jnp.repeat(w_in[:, :, main:SSD_IN], SSD_HEAD_DIM, axis=2).astype(BF16)
    return pl.pallas_call(
        _prep_w_in_kernel,
        grid=(depth, d // tr),
        in_specs=[pl.BlockSpec((None, tr, n), lambda l, i: (l, i, 0)),
                  pl.BlockSpec((None, tr, SSD_WIDTH), lambda l, i: (l, i, 0))],
        out_specs=pl.BlockSpec((None, tr, IN_COLS), lambda l, i: (l, i, 0)),
        out_shape=jax.ShapeDtypeStruct((depth, d, IN_COLS), BF16),
        compiler_params=pltpu.CompilerParams(dimension_semantics=("parallel", "parallel")),
        name="prep_w_in",
    )(w16, dt_rep)


def _in_proj_kernel(h_ref, nw_ref, w_ref, ssd_ref, rwkv_ref, lru_ref, ret_ref):
    tm = h_ref.shape[0]
    row = pl.program_id(1) * tm + lax.broadcasted_iota(jnp.int32, (tm, 1), 0)
    hn = _rms(h_ref[...], nw_ref[...])
    hn = jnp.where(row >= PAD, hn, 0.0).astype(BF16)
    o = 0
    for ref in (ssd_ref, rwkv_ref, lru_ref, ret_ref):
        n = ref.shape[1]
        ref[...] = jnp.dot(hn, w_ref[:, o:o + n], preferred_element_type=F32)
        o += n


def _in_proj(h, nw, w, layer, tm):
    bsz, tp, d = h.shape
    widths = (P_SSD_W, RWKV_IN, LRU_IN, RET_IN)
    return pl.pallas_call(
        _in_proj_kernel,
        grid=(bsz, tp // tm),
        in_specs=[pl.BlockSpec((None, tm, d), lambda b, i: (b, i, 0)),
                  pl.BlockSpec((1, d), lambda b, i: (0, 0)),
                  pl.BlockSpec((None, d, IN_COLS), lambda b, i: (layer, 0, 0))],
        out_specs=[pl.BlockSpec((None, tm, n), lambda b, i: (b, i, 0)) for n in widths],
        out_shape=[jax.ShapeDtypeStruct((bsz, tp, n), F32) for n in widths],
        compiler_params=pltpu.CompilerParams(
            dimension_semantics=("parallel", "arbitrary"), vmem_limit_bytes=VMEM_LIMIT_BYTES),
        name="in_proj",
    )(h, nw, w)


def _split_bf16(x, parts):
    out = []
    for _ in range(parts):
        hi = x.astype(BF16)
        out.append(hi)
        x = x - hi.astype(F32)
    return out


def _cumsum_rows(x, tril3_ref):
    return jnp.dot(tril3_ref[...], jnp.concatenate(_split_bf16(x, 3), axis=0), preferred_element_type=F32)


def _tril3(n):
    tril = (np.arange(n)[:, None] >= np.arange(n)[None, :]).astype(np.float32)
    return jnp.asarray(np.concatenate([tril] * 3, axis=1), BF16)


def _ssd_kernel(p_ref, cw_ref, cb_ref, dtb_ref, alog_ref, dsk_ref, nw_ref, tril3_ref, y_ref, xbuf, state):
    c = pl.program_id(1)
    nb = p_ref.shape[0]
    L = CHUNK

    @pl.when(c == 0)
    def _():
        xbuf[:, 0:8, :] = jnp.zeros((nb, 8, SSD_CONV_CH), F32)
        state[...] = jnp.zeros(state.shape, F32)

    row = c * L + lax.broadcasted_iota(jnp.int32, (L, 1), 0)
    causal = lax.broadcasted_iota(jnp.int32, (L, L), 0) >= lax.broadcasted_iota(jnp.int32, (L, L), 1)
    lane_lo = lax.broadcasted_iota(jnp.int32, (1, 128), 1) < SSD_HEAD_DIM
    groups = [slice(128 * g, 128 * g + 128) for g in range(2)]

    def prep(b):
        buf = xbuf.at[b]
        buf[8:8 + L, :] = p_ref[b, :, SSD_WIDTH:SSD_WIDTH + SSD_CONV_CH]
        xbc = _silu(_causal_conv(buf, cw_ref[...], cb_ref[...], L))
        buf[0:8, :] = buf[L:L + 8, :]
        dt = _softplus(p_ref[b, :, SSD_WIDTH + SSD_CONV_CH:P_SSD_W] + dtb_ref[...])
        dt = jnp.where(row >= PAD, dt, 0.0)
        a = dt * (-jnp.exp(alog_ref[...]))
        xs = xbc[:, 0:256]
        return dict(xs=xs, bs=xbc[:, 256:512], cs=xbc[:, 512:768], xdt=xs * dt,
                    acs=_cumsum_rows(a, tril3_ref), y=[None, None])

    def gram(d, g):
        sl = groups[g]
        return _mm_nt(d["cs"][:, sl], d["bs"][:, sl])

    def chunk(b, d, g, gmat):
        sl = groups[g]
        acs_g = d["acs"][:, sl]
        acs_t = acs_g.T
        a_last = acs_g[L - 1:L, :]
        b_g, c_g, x_g = d["bs"][:, sl], d["cs"][:, sl], d["xdt"][:, sl]
        ms = []
        for hh in range(2):
            o = SSD_HEAD_DIM * hh
            diff = acs_g[:, o:o + 1] - acs_t[o:o + 1, :]
            ms.append(gmat * jnp.exp(jnp.where(causal, diff, -1e30)))
        mcat = jnp.concatenate(ms, axis=1)
        xbd = jnp.concatenate([jnp.where(lane_lo, x_g, 0.0), jnp.where(lane_lo, 0.0, x_g)], axis=0)
        s_in = state[b, g]
        d["y"][g] = _mm(mcat, xbd) + _mm(c_g, s_in) * jnp.exp(acs_g)
        state[b, g] = s_in * jnp.exp(a_last) + _mm_tn(b_g, x_g * jnp.exp(a_last - acs_g))

    def finish(b, d, g):
        sl = groups[g]
        y_g = (d["y"][g] + dsk_ref[:, sl] * d["xs"][:, sl]) * _silu(p_ref[b, :, sl])
        y_ref[b, :, sl] = _rms(y_g, nw_ref[:, sl]).astype(y_ref.dtype)

    ds = [prep(b) for b in range(nb)]
    gm = [[gram(d, g) for g in range(2)] for d in ds]
    for b, d in enumerate(ds):
        for g in range(2):
            chunk(b, d, g, gm[b][g])
    for b, d in enumerate(ds):
        for g in range(2):
            finish(b, d, g)


def _ssd_mix(p, cw, cb, dtb, alog, dsk, nw, nb):
    bsz, tp, _ = p.shape
    vec = lambda n: pl.BlockSpec((1, n), lambda b, c: (0, 0))
    return pl.pallas_call(
        _ssd_kernel,
        grid=(bsz // nb, tp // CHUNK),
        in_specs=[pl.BlockSpec((nb, CHUNK, P_SSD_W), lambda b, c: (b, c, 0)),
                  pl.BlockSpec((SSD_CONV, SSD_CONV_CH), lambda b, c: (0, 0)),
                  vec(SSD_CONV_CH), vec(SSD_WIDTH), vec(SSD_WIDTH), vec(SSD_WIDTH), vec(SSD_WIDTH),
                  pl.BlockSpec((CHUNK, 3 * CHUNK), lambda b, c: (0, 0))],
        out_specs=pl.BlockSpec((nb, CHUNK, SSD_WIDTH), lambda b, c: (b, c, 0)),
        out_shape=jax.ShapeDtypeStruct((bsz, tp, SSD_WIDTH), BF16),
        scratch_shapes=[pltpu.VMEM((nb, CHUNK + 8, SSD_CONV_CH), F32),
                        pltpu.VMEM((nb, 2, SSD_STATE, 128), F32)],
        compiler_params=pltpu.CompilerParams(dimension_semantics=("parallel", "arbitrary")),
        name="ssd_mix",
    )(p, cw, cb, dtb, alog, dsk, nw, _tril3(CHUNK))


def _seg_sum(x, seg2_ref):
    return jnp.dot(jnp.concatenate(_split_bf16(x, 2), axis=1), seg2_ref[...], preferred_element_type=F32)


def _rwkv_kernel(p_ref, mu_ref, w0_ref, a0_ref, wa2_ref, g2_ref, kk_ref, ka_ref, rk_ref,
                 lnw_ref, lnb_ref, seg2_ref, tril3_ref, mask_ref, maskw_ref, y_ref, pbuf, state):
    c = pl.program_id(1)
    nb = p_ref.shape[0]
    C = RWKV_CHUNK
    W = RWKV_WIDTH
    R = RWKV_HEADS * C

    @pl.when(c == 0)
    def _():
        pbuf[:, 0:8, :] = jnp.zeros((nb, 8, RWKV_IN), F32)
        state[...] = jnp.zeros(state.shape, F32)

    lane = lax.broadcasted_iota(jnp.int32, (1, 128), 1)
    lane_head = lax.broadcasted_iota(jnp.int32, (1, W), 1) // RWKV_HEAD_DIM

    def stack(x):
        x = x.astype(BF16)
        zero = jnp.zeros_like(x)
        return jnp.concatenate([jnp.where(lane_head == h, x, zero) for h in range(RWKV_HEADS)], axis=0)

    def prep(b):
        pbuf[b, 8:8 + C, :] = p_ref[b]
        p = pbuf[b, 8:8 + C, :]
        p = p + (pbuf[b, 7:7 + C, :] - p) * mu_ref[...]
        pbuf[b, 0:8, :] = pbuf[b, C:C + 8, :]
        r, k, v = p[:, 0:W], p[:, W:2 * W], p[:, 2 * W:3 * W]
        lat = p[:, 3 * W:3 * W + 128]
        lat = jnp.where(lane < 64, jnp.tanh(lat), lat)
        wa = _mm(lat, wa2_ref[...])
        logw = -math.exp(-0.5) * _sigmoid(w0_ref[...] + wa[:, 0:W])
        a = _sigmoid(a0_ref[...] + wa[:, W:2 * W])
        g = _mm(_sigmoid(p[:, 3 * W + 128:3 * W + 256]), g2_ref[...])
        kk = k * kk_ref[...]
        kk = kk / jnp.maximum(jnp.sqrt(_seg_sum(kk * kk, seg2_ref)), 1e-12)
        k2 = k * (1.0 + (a - 1.0) * ka_ref[...])
        bonus = _seg_sum(r * k2 * rk_ref[...], seg2_ref) * v
        cum = _cumsum_rows(logw, tril3_ref)
        e_neg = jnp.exp(-cum)
        pc = jnp.exp(cum[C - 1:C, :])
        bt = kk * a * e_neg
        kt = k2 * e_neg
        return dict(
            lhs=jnp.concatenate([-kk * jnp.exp(cum - logw), r * jnp.exp(cum)], axis=0).astype(BF16),
            rhs=jnp.concatenate([stack(bt), stack(kt)], axis=0),
            upd=jnp.concatenate([bt * pc, kt * pc], axis=0).astype(BF16),
            v=v.astype(BF16), v_bd=stack(v), pc=pc, bonus=bonus, g=g)

    def scores(d):
        sc = lax.dot_general(d["lhs"], d["rhs"], (((1,), (1,)), ((), ())), preferred_element_type=F32)
        strict_w, incl_w = maskw_ref[0], maskw_ref[1]
        d["npow"] = sc[0:C, 0:R] * strict_w
        d["n_bd"] = stack(d["npow"])
        d["a_k"] = jnp.concatenate([sc[0:C, R:2 * R] * strict_w, sc[C:2 * C, R:2 * R] * incl_w],
                                   axis=0).astype(BF16)
        d["a_rb"] = (sc[C:2 * C, 0:R] * incl_w).astype(BF16)
        d["tinv"] = maskw_ref[2] + d["npow"]

    def double(d):
        d["npow"] = jnp.dot(d["npow"].astype(BF16), d["n_bd"], preferred_element_type=F32)
        d["n_bd"] = stack(d["npow"])
        d["tinv"] = d["tinv"] + jnp.dot(d["tinv"].astype(BF16), d["n_bd"], preferred_element_type=F32)

    def read_state(b, d):
        d["sxav"] = _mm_nt(d["lhs"], state[b]) + jnp.dot(d["a_k"], d["v_bd"], preferred_element_type=F32)

    def correction(d):
        d["u"] = jnp.dot(d["tinv"].astype(BF16), stack(d["sxav"][0:C]), preferred_element_type=F32)

    def write_state(b, d):
        u = d["u"]
        d["y"] = d["sxav"][C:2 * C] + jnp.dot(d["a_rb"], stack(u), preferred_element_type=F32)
        new = _mm_tn(jnp.concatenate([u.astype(BF16), d["v"]], axis=0), d["upd"])
        state[b] = state[b] * d["pc"] + new * mask_ref[...]

    def finish(b, d):
        inv_n = 1.0 / RWKV_HEAD_DIM
        yc = d["y"] - _seg_sum(d["y"], seg2_ref) * inv_n
        var = _seg_sum(yc * yc, seg2_ref) * inv_n
        yn = yc * lax.rsqrt(var + RWKV_GN_EPS) * lnw_ref[...] + lnb_ref[...]
        y_ref[b] = ((yn + d["bonus"]) * d["g"]).astype(y_ref.dtype)

    ds = [prep(b) for b in range(nb)]
    for d in ds:
        scores(d)
    n = 1
    while 2 * n < C:
        for d in ds:
            double(d)
        n *= 2
    for b, d in enumerate(ds):
        read_state(b, d)
    for d in ds:
        correction(d)
    for b, d in enumerate(ds):
        write_state(b, d)
    for b, d in enumerate(ds):
        finish(b, d)


def _rwkv_tables():
    C, W, R = RWKV_CHUNK, RWKV_WIDTH, RWKV_HEADS * RWKV_CHUNK
    head = np.arange(W) // RWKV_HEAD_DIM
    seg = (head[:, None] == head[None, :]).astype(np.float32)
    assert C == RWKV_HEAD_DIM
    cc = np.arange(R)[None, :]
    tt = np.arange(C)[:, None]
    wide = np.stack([(cc % C) < tt, (cc % C) <= tt, (cc % C) == tt])
    return (jnp.asarray(np.concatenate([seg, seg], axis=0), BF16), _tril3(C), jnp.asarray(seg, F32),
            jnp.asarray(wide.astype(np.float32), F32))


def _rwkv_mix(p, mu, w0, a0, wa2, g2, kk, ka, rk, lnw, lnb, nb):
    bsz, tp, _ = p.shape
    C, W, R = RWKV_CHUNK, RWKV_WIDTH, RWKV_HEADS * RWKV_CHUNK
    vec = lambda n: pl.BlockSpec((1, n), lambda b, c: (0, 0))
    full = lambda s: pl.BlockSpec(s, lambda b, c: (0,) * len(s))
    return pl.pallas_call(
        _rwkv_kernel,
        grid=(bsz // nb, tp // C),
        in_specs=[pl.BlockSpec((nb, C, RWKV_IN), lambda b, c: (b, c, 0)),
                  vec(RWKV_IN), vec(W), vec(W), full((128, 2 * W)), full((128, W)),
                  vec(W), vec(W), vec(W), vec(W), vec(W),
                  full((2 * W, W)), full((C, 3 * C)), full((W, W)), full((3, C, R))],
        out_specs=pl.BlockSpec((nb, C, W), lambda b, c: (b, c, 0)),
        out_shape=jax.ShapeDtypeStruct((bsz, tp, W), BF16),
        scratch_shapes=[pltpu.VMEM((nb, C + 8, RWKV_IN), F32),
                        pltpu.VMEM((nb, W, W), F32)],
        compiler_params=pltpu.CompilerParams(dimension_semantics=("parallel", "arbitrary")),
        name="rwkv_mix",
    )(p, mu, w0, a0, wa2, g2, kk, ka, rk, lnw, lnb, *_rwkv_tables())


def _lru_kernel(p_ref, cw_ref, cb_ref, wax_ref, bax_ref, lam_ref, y_ref, xbuf, hprev):
    c = pl.program_id(1)
    nb = p_ref.shape[0]
    L = CHUNK
    W = LRU_WIDTH

    @pl.when(c == 0)
    def _():
        xbuf[:, 0:8, :] = jnp.zeros((nb, 8, W), F32)
        hprev[...] = jnp.zeros(hprev.shape, F32)

    rowi = lax.broadcasted_iota(jnp.int32, (L, 1), 0)
    log_a_unit = -LRU_C * _softplus(-lam_ref[...])
    for b in range(nb):
        buf = xbuf.at[b]
        buf[8:8 + L, :] = p_ref[b, :, 0:W]
        xc = _causal_conv(buf, cw_ref[...], cb_ref[...], L)
        buf[0:8, :] = buf[L:L + 8, :]
        gates = _sigmoid(_mm(xc, wax_ref[...]) + bax_ref[...])
        r, i = gates[:, 0:W], gates[:, W:2 * W]
        log_a = r * log_a_unit
        a = jnp.exp(log_a)
        t = jnp.tanh(log_a)
        one_minus_a2 = -2.0 * t / (1.0 - t)
        u = jnp.sqrt(one_minus_a2) * (i * xc)
        u = jnp.where(c * L + rowi >= PAD, u, 0.0)

        s = 1
        while s < 8:
            keep = (rowi & 7) >= s
            a_sh = jnp.where(keep, pltpu.roll(a, s, axis=0), 1.0)
            u_sh = jnp.where(keep, pltpu.roll(u, s, axis=0), 0.0)
            u = u + a * u_sh
            a = a * a_sh
            s *= 2
        carry = hprev[b, 0:1, :]
        groups = []
        for i in range(L // 8):
            h_i = u[8 * i:8 * i + 8, :] + a[8 * i:8 * i + 8, :] * carry
            groups.append(h_i)
            carry = h_i[7:8, :]
        hprev[b, 0:1, :] = carry
        h = jnp.concatenate(groups, axis=0)
        y_ref[b] = (h * jax.nn.gelu(p_ref[b, :, W:2 * W], approximate=True)).astype(y_ref.dtype)


def _lru_mix(p, cw, cb, wax, bax, lam, nb):
    bsz, tp, _ = p.shape
    vec = lambda n: pl.BlockSpec((1, n), lambda b, c: (0, 0))
    return pl.pallas_call(
        _lru_kernel,
        grid=(bsz // nb, tp // CHUNK),
        in_specs=[pl.BlockSpec((nb, CHUNK, LRU_IN), lambda b, c: (b, c, 0)),
                  pl.BlockSpec((LRU_CONV, LRU_WIDTH), lambda b, c: (0, 0)),
                  vec(LRU_WIDTH),
                  pl.BlockSpec((LRU_WIDTH, 2 * LRU_WIDTH), lambda b, c: (0, 0)),
                  vec(2 * LRU_WIDTH), vec(LRU_WIDTH)],
        out_specs=pl.BlockSpec((nb, CHUNK, LRU_WIDTH), lambda b, c: (b, c, 0)),
        out_shape=jax.ShapeDtypeStruct((bsz, tp, LRU_WIDTH), BF16),
        scratch_shapes=[pltpu.VMEM((nb, CHUNK + 8, LRU_WIDTH), F32),
                        pltpu.VMEM((nb, 8, LRU_WIDTH), F32)],
        compiler_params=pltpu.CompilerParams(dimension_semantics=("parallel", "arbitrary")),
        name="lru_mix",
    )(p, cw, cb, wax, bax, lam)


def _ret_kernel(p_ref, cos_ref, sin_ref, dmat_ref, kdec_ref, qdec_ref, cross_ref, bd_ref, seg2_ref,
                gnw_ref, y_ref, state):
    c = pl.program_id(1)
    nb = p_ref.shape[0]
    W = RET_WIDTH

    @pl.when(c == 0)
    def _():
        state[...] = jnp.zeros(state.shape, F32)

    lane = lax.broadcasted_iota(jnp.int32, (1, RET_QK), 1)
    first_half = (lane % RET_QK_DIM) < (RET_QK_DIM // 2)
    half = RET_QK_DIM // 2
    qk_head = lane // RET_QK_DIM
    v_head = lax.broadcasted_iota(jnp.int32, (1, W), 1) // RET_V_DIM

    def rope(x):
        swapped = jnp.where(first_half, pltpu.roll(x, RET_QK - half, axis=1), pltpu.roll(x, half, axis=1))
        return x * cos_ref[...] + swapped * sin_ref[...]

    def prep(b):
        q = rope(p_ref[b, :, 0:RET_QK])
        k = rope(p_ref[b, :, RET_QK:2 * RET_QK]) * (RET_QK_DIM ** -0.5)
        v = p_ref[b, :, 2 * RET_QK:2 * RET_QK + W]
        return dict(q=q, k32=k, k=k.astype(BF16), v=v.astype(BF16), v32=v)

    def scores(d):
        d["s"] = [(_mm_nt(jnp.where(qk_head == h, d["q"], 0.0), d["k"]) * dmat_ref[h]).astype(BF16)
                  for h in range(RET_HEADS)]

    def outputs(b, d):
        y = _mm(d["q"] * qdec_ref[...], state[b])
        for h in range(RET_HEADS):
            y = y + jnp.dot(d["s"][h], jnp.where(v_head == h, d["v32"], 0.0).astype(BF16),
                            preferred_element_type=F32)
        d["y"] = y
        kv = _mm_tn(d["k32"] * kdec_ref[...], d["v"]) * bd_ref[...]
        state[b] = state[b] * cross_ref[...] + kv

    def center(d):
        d["yc"] = d["y"] - _seg_sum(d["y"], seg2_ref) * (1.0 / RET_V_DIM)

    def finish(b, d):
        var = _seg_sum(d["yc"] * d["yc"], seg2_ref) * (1.0 / RET_V_DIM)
        g = p_ref[b, :, 2 * RET_QK + W:2 * RET_QK + 2 * W]
        y_ref[b] = (d["yc"] * lax.rsqrt(var + RET_GN_EPS) * gnw_ref[...] * _silu(g)).astype(y_ref.dtype)

    ds = [prep(b) for b in range(nb)]
    for d in ds:
        scores(d)
    for b, d in enumerate(ds):
        outputs(b, d)
    for d in ds:
        center(d)
    for b, d in enumerate(ds):
        finish(b, d)


def _ret_mix(p, cos_t, sin_t, dmat, kdec, qdec, cross, bd, seg2, gnw, nb):
    bsz, tp, _ = p.shape
    const2 = lambda s: pl.BlockSpec(s, lambda b, c: (0, 0))
    return pl.pallas_call(
        _ret_kernel,
        grid=(bsz // nb, tp // CHUNK),
        in_specs=[pl.BlockSpec((nb, CHUNK, RET_IN), lambda b, c: (b, c, 0)),
                  pl.BlockSpec((CHUNK, RET_QK), lambda b, c: (c, 0)),
                  pl.BlockSpec((CHUNK, RET_QK), lambda b, c: (c, 0)),
                  pl.BlockSpec((RET_HEADS, CHUNK, CHUNK), lambda b, c: (0, 0, 0)),
                  const2((CHUNK, RET_QK)), const2((CHUNK, RET_QK)),
                  const2((RET_QK, RET_WIDTH)), const2((RET_QK, RET_WIDTH)),
                  const2((2 * RET_WIDTH, RET_WIDTH)), const2((1, RET_WIDTH))],
        out_specs=pl.BlockSpec((nb, CHUNK, RET_WIDTH), lambda b, c: (b, c, 0)),
        out_shape=jax.ShapeDtypeStruct((bsz, tp, RET_WIDTH), BF16),
        scratch_shapes=[pltpu.VMEM((nb, RET_QK, RET_WIDTH), F32)],
        compiler_params=pltpu.CompilerParams(dimension_semantics=("parallel", "arbitrary")),
        name="ret_mix",
    )(p, cos_t, sin_t, dmat, kdec, qdec, cross, bd, seg2, gnw)


def _retention_tables(tp):
    half = RET_QK_DIM // 2
    lane = np.arange(RET_QK)
    freqs = ROPE_BASE ** (-np.arange(half, dtype=np.float64) / half)
    pos = np.arange(tp, dtype=np.float64) - PAD
    ang = pos[:, None] * freqs[lane % half][None, :]
    sign = np.where((lane % RET_QK_DIM) < half, -1.0, 1.0)
    cos_t = np.cos(ang)
    sin_t = np.sin(ang) * sign[None, :]
    log_g = np.log1p(-np.exp2(-5.0 - np.arange(RET_HEADS, dtype=np.float64)))
    idx = np.arange(CHUNK)
    rel = idx[:, None] - idx[None, :]
    dmat = np.where(rel >= 0, np.exp(np.maximum(rel, 0)[None] * log_g[:, None, None]), 0.0)
    lg_lane = log_g[lane // RET_QK_DIM]
    kdec = np.exp((CHUNK - 1 - idx)[:, None] * lg_lane[None, :])
    qdec = np.exp((idx + 1)[:, None] * lg_lane[None, :])
    cross = np.broadcast_to(np.exp(CHUNK * lg_lane)[:, None], (RET_QK, RET_WIDTH))
    v_head = np.arange(RET_WIDTH) // RET_V_DIM
    bd = (lane // RET_QK_DIM)[:, None] == v_head[None, :]
    seg = (v_head[:, None] == v_head[None, :]).astype(np.float32)
    f = lambda x: jnp.asarray(np.ascontiguousarray(x), F32)
    return (f(cos_t), f(sin_t), f(dmat), f(kdec), f(qdec), f(cross), f(bd),
            jnp.asarray(np.concatenate([seg, seg], axis=0), BF16))


def _out_ffn_kernel(h_ref, y0_ref, y1_ref, y2_ref, y3_ref, wo_ref, n1_ref, n2_ref, n3_ref,
                    wg_ref, wu_ref, wd_ref, o_ref, *, tf):
    u = None
    for i, yr in enumerate((y0_ref, y1_ref, y2_ref, y3_ref)):
        part = _mm(yr[...], wo_ref[256 * i:256 * i + 256, :])
        u = part if u is None else u + part
    h1 = h_ref[...] + _rms(u, n1_ref[...])
    hn = _rms(h1, n2_ref[...]).astype(BF16)
    acts = []
    for j in range(D_FF // tf):
        sl = slice(j * tf, (j + 1) * tf)
        act = _silu(jnp.dot(hn, wg_ref[:, sl], preferred_element_type=F32)) * \
            jnp.dot(hn, wu_ref[:, sl], preferred_element_type=F32)
        acts.append(act.astype(BF16))
    acc = jnp.dot(jnp.concatenate(acts, axis=1), wd_ref[...], preferred_element_type=F32)
    o_ref[...] = h1 + _rms(acc, n3_ref[...])


def _out_ffn(h, ys, wo, n1, n2, n3, wg, wu, wd, layer, tm, tf):
    m, d = h.shape
    row = lambda n: pl.BlockSpec((tm, n), lambda i: (i, 0))
    full = lambda s: pl.BlockSpec(s, lambda i: (0, 0))
    stacked = lambda s: pl.BlockSpec((None,) + s, lambda i: (layer, 0, 0))
    return pl.pallas_call(
        functools.partial(_out_ffn_kernel, tf=tf),
        grid=(m // tm,),
        in_specs=[row(d), row(256), row(256), row(256), row(256),
                  stacked((d, d)), full((1, d)), full((1, d)), full((1, d)),
                  stacked((d, D_FF)), stacked((d, D_FF)), stacked((D_FF, d))],
        out_specs=row(d),
        out_shape=jax.ShapeDtypeStruct((m, d), F32),
        compiler_params=pltpu.CompilerParams(
            dimension_semantics=("parallel",), vmem_limit_bytes=VMEM_LIMIT_BYTES),
        name="out_ffn",
    )(h, *ys, wo, n1, n2, n3, wg, wu, wd)


def _block_diag(blocks):
    g, n, m = blocks.shape
    eye = jnp.eye(g, dtype=blocks.dtype)
    return (eye[:, None, :, None] * blocks[:, :, None, :]).reshape(g * n, g * m)


def _row_tile(tp):
    best = 8
    for t in range(8, 1153, 8):
        if tp % t == 0:
            best = t
    return best


def kernel(x, meta_tokens, pre_mix_norm, post_mix_norm, pre_ffn_norm, post_ffn_norm, w_in, w_out, ssd_conv_w, ssd_conv_b, ssd_dt_bias, ssd_a_log, ssd_d, ssd_norm_w, rwkv_mu, rwkv_w0, rwkv_w2, rwkv_a0, rwkv_a2, rwkv_g2, rwkv_k_k, rwkv_k_a, rwkv_r_k, rwkv_ln_w, rwkv_ln_b, lru_conv_w, lru_conv_b, lru_wa, lru_ba, lru_wx, lru_bx, lru_lambda, ret_gn_w, ffn_w_gate, ffn_w_up, ffn_w_down):
    bsz, seq, d = x.shape
    depth = w_in.shape[0]
    t = N_META + seq
    tp = t + PAD
    assert d == D_MODEL and tp % CHUNK == 0
    meta = jnp.broadcast_to(meta_tokens.astype(x.dtype)[None], (bsz, N_META, d))
    h = jnp.concatenate([jnp.zeros((bsz, PAD, d), x.dtype), meta, x], axis=1)

    tm_in = _row_tile(tp)
    m_rows = bsz * tp
    tm_ffn = 512 if m_rows % 512 == 0 else CHUNK
    tf = 256
    tables = _retention_tables(tp)
    per_head = lambda v: jnp.repeat(v, SSD_HEAD_DIM, axis=-1)[None]
    r2 = lambda v: v[None]

    w_cat = _prep_w_in(w_in)
    nb = next(n for n in (8, 4, 2, 1) if bsz % n == 0)
    nb_rwkv = nb
    wo16, wg16, wu16, wd16 = (w.astype(BF16) for w in (w_out, ffn_w_gate, ffn_w_up, ffn_w_down))

    for l in range(depth):
        p_ssd, p_rwkv, p_lru, p_ret = _in_proj(h, r2(pre_mix_norm[l]), w_cat, l, tm_in)

        y_ssd = _ssd_mix(p_ssd, ssd_conv_w[l], r2(ssd_conv_b[l]), per_head(ssd_dt_bias[l]),
                         per_head(ssd_a_log[l]), per_head(ssd_d[l]), r2(ssd_norm_w[l]), nb)

        zero = jnp.zeros((64, RWKV_WIDTH), F32)
        wa2 = jnp.concatenate([jnp.concatenate([rwkv_w2[l], zero], axis=1),
                               jnp.concatenate([zero, rwkv_a2[l]], axis=1)], axis=0).astype(BF16)
        y_rwkv = _rwkv_mix(p_rwkv, r2(rwkv_mu[l]), r2(rwkv_w0[l]), r2(rwkv_a0[l]), wa2,
                           rwkv_g2[l].astype(BF16), r2(rwkv_k_k[l]), r2(rwkv_k_a[l]),
                           rwkv_r_k[l].reshape(1, RWKV_WIDTH), r2(rwkv_ln_w[l]), r2(rwkv_ln_b[l]),
                           nb_rwkv)

        wax = jnp.concatenate([_block_diag(lru_wa[l]), _block_diag(lru_wx[l])], axis=1).astype(BF16)
        bax = jnp.concatenate([lru_ba[l], lru_bx[l]])[None]
        y_lru = _lru_mix(p_lru, lru_conv_w[l], r2(lru_conv_b[l]), wax, bax, r2(lru_lambda[l]), nb)

        y_ret = _ret_mix(p_ret, *tables, r2(ret_gn_w[l]), nb)

        flat = lambda a: a.reshape(m_rows, a.shape[-1])
        h = _out_ffn(flat(h), [flat(y_ssd), flat(y_rwkv), flat(y_lru), flat(y_ret)],
                     wo16, r2(post_mix_norm[l]), r2(pre_ffn_norm[l]), r2(post_ffn_norm[l]),
                     wg16, wu16, wd16, l, tm_ffn, tf).reshape(bsz, tp, d)
    return h[:, PAD + N_META:]
```
